```python
import math
import jax, jax.numpy as jnp
from jax import lax
import numpy as np

D_MODEL = 2048
BATCH = 2
SEQ = 4096
DEPTH = 1
DEC_BATCH = 128
DEC_SEQ = 8
PAST_LEN = 2048
PAGE_SIZE = 128

D_ATT = D_MODEL // 2
HD_A = 64
H_A = D_ATT // (2 * HD_A)
H_M = 4
D_MV = D_MODEL // 2
DV_M = D_MV // H_M
DK_M = DV_M // 2
D_MQK = H_M * DK_M
CHUNK = 64
QBLK = 128
EPS = 1e-6
SPLITS = (D_ATT, D_ATT, D_ATT, D_ATT, D_MQK, D_MQK, D_MV, D_MV, D_MV, H_M, H_M, D_MODEL, D_MODEL)
D_IN = 4 * D_ATT + 2 * D_MQK + 3 * D_MV + 2 * H_M + 2 * D_MODEL

kernel_name = 'hybrid_diffattn_mlstm_step'


def rmsnorm(x, g):
    xf = x.astype(jnp.float32)
    xf = xf * lax.rsqrt(jnp.mean(xf * xf, axis=-1, keepdims=True) + EPS)
    return (xf * g.astype(jnp.float32)).astype(x.dtype)


def split_in(u):
    idx = [int(i) for i in np.cumsum(SPLITS)[:-1]]
    return jnp.split(u, idx, axis=-1)


def alibi_slopes():
    return 2.0 ** (-8.0 * jnp.arange(1, H_A + 1, dtype=jnp.float32) / H_A)


def diff_attention(q, k, v, q_pos, k_pos, lam):
    s = jnp.einsum('bqhcd,bkhcd->bhcqk', q.astype(jnp.float32), k.astype(jnp.float32)) * (HD_A ** -0.5)
    dist = (q_pos[:, None] - k_pos[None, :]).astype(jnp.float32)
    bias = -alibi_slopes()[:, None, None, None] * dist[None, None]
    s = jnp.where(k_pos[None, :] <= q_pos[:, None], s + bias, -jnp.inf)
    p = jax.nn.softmax(s, axis=-1)
    a = p[:, :, 0] - lam * p[:, :, 1]
    return jnp.einsum('bhqk,bkhe->bqhe', a.astype(v.dtype), v)


def mlstm_chunk(carry, xs):
    C, n, m = carry
    q, k, v, ig, lf = xs
    L = q.shape[2]
    b = jnp.cumsum(lf, axis=-1)
    a = b + m[..., None]
    causal = jnp.tril(jnp.ones((L, L), dtype=bool))
    dmat = jnp.where(causal, b[..., :, None] - b[..., None, :] + ig[..., None, :], -jnp.inf)
    m_t = jnp.maximum(a, jnp.max(dmat, axis=-1))
    s = jnp.einsum('bhtk,bhsk->bhts', q, k) * jnp.exp(dmat - m_t[..., None])
    inter = jnp.exp(a - m_t)
    num = inter[..., None] * jnp.einsum('bhtk,bhvk->bhtv', q, C) + jnp.einsum('bhts,bhsv->bhtv', s, v)
    den = inter * jnp.einsum('bhtk,bhk->bht', q, n) + jnp.sum(s, axis=-1)
    h = num / jnp.maximum(jnp.abs(den), jnp.exp(-m_t))[..., None]
    m_new = m_t[..., -1]
    wk = jnp.exp(b[..., -1:] - b + ig - m_new[..., None])
    decay = jnp.exp(a[..., -1] - m_new)
    C_new = decay[..., None, None] * C + jnp.einsum('bhs,bhsv,bhsk->bhvk', wk, v, k)
    n_new = decay[..., None] * n + jnp.einsum('bhs,bhsk->bhk', wk, k)
    return (C_new, n_new, m_new), h


def mlstm(q, k, v, ig, lf, C0, n0, m0):
    B, T = q.shape[:2]
    L = math.gcd(T, CHUNK)
    nc = T // L

    def to_chunks(x):
        x = jnp.moveaxis(x.astype(jnp.float32), 2, 1)
        x = x.reshape(x.shape[:2] + (nc, L) + x.shape[3:])
        return jnp.moveaxis(x, 2, 0)

    xs = (to_chunks(q), to_chunks(k), to_chunks(v), to_chunks(ig), to_chunks(lf))
    carry0 = (C0.astype(jnp.float32), n0.astype(jnp.float32), m0.astype(jnp.float32))
    (C1, n1, m1), h = lax.scan(mlstm_chunk, carry0, xs)
    h = jnp.moveaxis(h, 0, 2).reshape(B, H_M, T, DV_M)
    return jnp.transpose(h, (0, 2, 1, 3)).astype(q.dtype), C1, n1, m1


def attend_prompt(q, k_rows, v_rows, lam):
    B, T = q.shape[:2]
    nb = T // QBLK
    k = k_rows.reshape(B, T, H_A, 2, HD_A)
    pos = jnp.arange(T, dtype=jnp.int32)
    qb = jnp.moveaxis(q.reshape(B, nb, QBLK, H_A, 2, HD_A), 1, 0)
    pb = pos.reshape(nb, QBLK)
    out = lax.map(lambda args: diff_attention(args[0], k, v_rows, args[1], pos, lam), (qb, pb))
    return jnp.moveaxis(out, 0, 1).reshape(B, T, H_A, 2 * HD_A)


def attend_sample(q, k_rows, v_rows, lam, cache_k, cache_v, page_table):
    Bd, T = q.shape[:2]
    past_len = page_table.shape[1] * PAGE_SIZE
    k_past = cache_k[page_table].reshape(Bd, past_len, H_A, 2 * HD_A)
    v_past = cache_v[page_table].reshape(Bd, past_len, H_A, 2 * HD_A)
    k_all = jnp.concatenate([k_past.astype(k_rows.dtype), k_rows], axis=1).reshape(Bd, past_len + T, H_A, 2, HD_A)
    v_all = jnp.concatenate([v_past.astype(v_rows.dtype), v_rows], axis=1)
    q_pos = past_len + jnp.arange(T, dtype=jnp.int32)
    k_pos = jnp.arange(past_len + T, dtype=jnp.int32)
    return diff_attention(q, k_all, v_all, q_pos, k_pos, lam)


def sublayer(x, c, attend, C0, n0, m0, lam, lam_init, w_ada, b_ada, norm_pre, norm_post, w_in,
             b_igate, b_fgate, attn_head_norm, mlstm_head_norm, w_br_a, w_br_m, w_out):
    B, T, _ = x.shape
    mod = jax.nn.silu(c) @ w_ada + b_ada
    shift, scale, gate = jnp.split(mod[:, None, :], 3, axis=-1)
    h = rmsnorm(x, norm_pre) * (1 + scale) + shift
    u = h @ w_in
    q_a, k_a, v_a, z_a, q_m, k_m, v_m, o_m, z_m, i_m, f_m, g_a, g_m = split_in(u)
    q_a = q_a.reshape(B, T, H_A, 2, HD_A)
    k_rows = k_a.reshape(B, T, H_A, 2 * HD_A)
    v_rows = v_a.reshape(B, T, H_A, 2 * HD_A)
    att = attend(q_a, k_rows, v_rows, lam)
    att = rmsnorm(att, attn_head_norm.reshape(H_A, 2 * HD_A)) * (1 - lam_init)
    y_a = (att.reshape(B, T, D_ATT) * jax.nn.silu(z_a)) @ w_br_a
    qm = q_m.reshape(B, T, H_M, DK_M)
    km = k_m.reshape(B, T, H_M, DK_M) * (DK_M ** -0.5)
    vm = v_m.reshape(B, T, H_M, DV_M)
    ig = (i_m + b_igate).astype(jnp.float32)
    lf = jax.nn.log_sigmoid((f_m + b_fgate).astype(jnp.float32))
    hm, C1, n1, m1 = mlstm(qm, km, vm, ig, lf, C0, n0, m0)
    hm = rmsnorm(hm, mlstm_head_norm.reshape(H_M, DV_M)).reshape(B, T, D_MV)
    y_m = (jax.nn.sigmoid(o_m) * hm * jax.nn.silu(z_m)) @ w_br_m
    y = (jax.nn.sigmoid(g_a) * y_a + jax.nn.sigmoid(g_m) * y_m) @ w_out
    out = x + gate * rmsnorm(y, norm_post)
    return out, k_rows, v_rows, C1, n1, m1


def setup_inputs(seed: int = 0) -> dict:
    key = jax.random.key(seed)
    ks = jax.random.split(key, 28)
    n_pages = PAST_LEN // PAGE_SIZE
    n_used = DEC_BATCH * n_pages
    n_pool = n_used + max(1, n_used // 4)
    page_table = jax.random.permutation(ks[0], n_pool)[:n_used].astype(jnp.int32).reshape(DEC_BATCH, n_pages)
    nrm = jax.random.normal
    f32 = jnp.float32
    return {
        'x_prompt': nrm(ks[1], (BATCH, SEQ, D_MODEL), f32),
        'x_sample': nrm(ks[2], (DEC_BATCH, DEC_SEQ, D_MODEL), f32),
        'cache_k': nrm(ks[3], (DEPTH, n_pool, PAGE_SIZE, H_A, 2 * HD_A), f32),
        'cache_v': nrm(ks[4], (DEPTH, n_pool, PAGE_SIZE, H_A, 2 * HD_A), f32),
        'state_C': 0.1 * nrm(ks[5], (DEPTH, DEC_BATCH, H_M, DV_M, DK_M), f32),
        'state_n': 0.5 * nrm(ks[6], (DEPTH, DEC_BATCH, H_M, DK_M), f32),
        'state_m': nrm(ks[7], (DEPTH, DEC_BATCH, H_M), f32),
        'page_table': page_table,
        'c_prompt': nrm(ks[8], (BATCH, D_MODEL), f32),
        'c_sample': nrm(ks[9], (DEC_BATCH, D_MODEL), f32),
        'w_ada': 0.5 * D_MODEL ** -0.5 * nrm(ks[10], (DEPTH, D_MODEL, 3 * D_MODEL), f32),
        'b_ada': 0.02 * nrm(ks[11], (DEPTH, 3 * D_MODEL), f32),
        'norm_pre': 1.0 + 0.05 * nrm(ks[12], (DEPTH, D_MODEL), f32),
        'norm_post': 1.0 + 0.05 * nrm(ks[13], (DEPTH, D_MODEL), f32),
        'w_in': D_MODEL ** -0.5 * nrm(ks[14], (DEPTH, D_MODEL, D_IN), f32),
        'b_igate': 0.1 * nrm(ks[15], (DEPTH, H_M), f32),
        'b_fgate': jnp.linspace(3.0, 6.0, H_M, dtype=f32) + 0.1 * nrm(ks[16], (DEPTH, H_M), f32),
        'lambda_q1': 0.1 * nrm(ks[17], (DEPTH, HD_A), f32),
        'lambda_k1': 0.1 * nrm(ks[18], (DEPTH, HD_A), f32),
        'lambda_q2': 0.1 * nrm(ks[19], (DEPTH, HD_A), f32),
        'lambda_k2': 0.1 * nrm(ks[20], (DEPTH, HD_A), f32),
        'attn_head_norm': 1.0 + 0.05 * nrm(ks[21], (DEPTH, D_ATT), f32),
        'mlstm_head_norm': 1.0 + 0.05 * nrm(ks[22], (DEPTH, D_MV), f32),
        'w_br_a': D_ATT ** -0.5 * nrm(ks[23], (DEPTH, D_ATT, D_MODEL), f32),
        'w_br_m': D_MV ** -0.5 * nrm(ks[24], (DEPTH, D_MV, D_MODEL), f32),
        'w_out': D_MODEL ** -0.5 * nrm(ks[25], (DEPTH, D_MODEL, D_MODEL), f32),
    }


def reference(x_prompt, x_sample, cache_k, cache_v, state_C, state_n, state_m, page_table, c_prompt, c_sample,
              w_ada, b_ada, norm_pre, norm_post, w_in, b_igate, b_fgate, lambda_q1, lambda_k1, lambda_q2,
              lambda_k2, attn_head_norm, mlstm_head_norm, w_br_a, w_br_m, w_out):
    xp, xs = x_prompt, x_sample
    kp_l, vp_l, Cp_l, np_l, mp_l = [], [], [], [], []
    ks_l, vs_l, Cs_l, ns_l, ms_l = [], [], [], [], []
    for layer in range(DEPTH):
        lam_init = 0.8 - 0.6 * math.exp(-0.3 * layer)
        lam = (jnp.exp(jnp.dot(lambda_q1[layer], lambda_k1[layer]).astype(jnp.float32))
               - jnp.exp(jnp.dot(lambda_q2[layer], lambda_k2[layer]).astype(jnp.float32)) + lam_init)
        weights = (w_ada[layer], b_ada[layer], norm_pre[layer], norm_post[layer], w_in[layer], b_igate[layer],
                   b_fgate[layer], attn_head_norm[layer], mlstm_head_norm[layer], w_br_a[layer], w_br_m[layer],
                   w_out[layer])
        B = xp.shape[0]
        C0 = jnp.zeros((B, H_M, DV_M, DK_M), jnp.float32)
        n0 = jnp.zeros((B, H_M, DK_M), jnp.float32)
        m0 = jnp.zeros((B, H_M), jnp.float32)
        xp, kp, vp, Cp, n_p, mp = sublayer(xp, c_prompt, attend_prompt, C0, n0, m0, lam, lam_init, *weights)
        ck, cv = cache_k[layer], cache_v[layer]
        att_s = lambda q, k, v, lm: attend_sample(q, k, v, lm, ck, cv, page_table)
        xs, k_s, v_s, Cs, n_s, ms = sublayer(xs, c_sample, att_s, state_C[layer], state_n[layer], state_m[layer],
                                             lam, lam_init, *weights)
        kp_l.append(kp); vp_l.append(vp); Cp_l.append(Cp); np_l.append(n_p); mp_l.append(mp)
        ks_l.append(k_s); vs_l.append(v_s); Cs_l.append(Cs); ns_l.append(n_s); ms_l.append(ms)
    return (xp, xs,
            jnp.stack(kp_l), jnp.stack(vp_l), jnp.stack(Cp_l), jnp.stack(np_l), jnp.stack(mp_l),
            jnp.stack(ks_l), jnp.stack(vs_l), jnp.stack(Cs_l), jnp.stack(ns_l), jnp.stack(ms_l))
```

```python
import functools
import math

import jax
import jax.numpy as jnp
import numpy as np
from jax import lax
from jax.experimental import pallas as pl
from jax.experimental.pallas import tpu as pltpu

F32 = jnp.float32
BF16 = jnp.bfloat16
EPS = 1e-6
NEG = -1e30
VMEM_LIMIT = 56 * 1024 * 1024
LANE = 128
MLSTM_CHUNK = 256
ATT_BLOCK = 512
PAGES_PER_STEP = 8
MLSTM_DEC_BB = 8

_NT = (((1,), (1,)), ((), ()))
_TN = (((0,), (0,)), ((), ()))


def _cparams(sem):
    return pltpu.CompilerParams(dimension_semantics=sem, vmem_limit_bytes=VMEM_LIMIT)


def _sigmoid(x):
    return 1.0 / (1.0 + jnp.exp(-x))


def _silu(x):
    return x * _sigmoid(x)


def _mod_kernel(c_ref, w_ref, b_ref, o_ref):
    a = _silu(c_ref[...]).astype(BF16)
    o_ref[...] = jnp.dot(a, w_ref[...].astype(BF16), preferred_element_type=F32) + b_ref[...]


def _mod_call(c_all, w_ada, b_ada):
    rows, d = c_all.shape
    n = w_ada.shape[1]
    tn = 512
    return pl.pallas_call(
        _mod_kernel,
        grid=(n // tn,),
        in_specs=[pl.BlockSpec((rows, d), lambda j: (0, 0)),
                  pl.BlockSpec((d, tn), lambda j: (0, j)),
                  pl.BlockSpec((1, tn), lambda j: (0, j))],
        out_specs=pl.BlockSpec((rows, tn), lambda j: (0, j)),
        out_shape=jax.ShapeDtypeStruct((rows, n), F32),
        compiler_params=_cparams(("parallel",)),
        name="mod",
    )(c_all, w_ada, b_ada)


def _inproj_a_kernel(x_ref, sc_ref, sh_ref, g_ref, w_ref, u_ref, h_ref, *, rowwise):
    @pl.when(pl.program_id(1) == 0)
    def _():
        x = x_ref[...]
        xn = x * lax.rsqrt(jnp.mean(x * x, axis=-1, keepdims=True) + EPS) * g_ref[...]
        sc = sc_ref[...] if rowwise else sc_ref[0]
        sh = sh_ref[...] if rowwise else sh_ref[0]
        h_ref[...] = (xn * (1.0 + sc) + sh).astype(BF16)

    u_ref[...] = jnp.dot(h_ref[...], w_ref[...], preferred_element_type=F32).astype(u_ref.dtype)


def _inproj_a_call(x, scale, shift, norm_pre, w, *, rowwise, rows_per_group, tm, tn, u_dtype):
    t, d = x.shape
    n = w.shape[1]
    if rowwise:
        mspec = pl.BlockSpec((tm, d), lambda i, j: (i, 0))
    else:
        mspec = pl.BlockSpec((1, 1, d), lambda i, j: ((i * tm) // rows_per_group, 0, 0))
    return pl.pallas_call(
        functools.partial(_inproj_a_kernel, rowwise=rowwise),
        grid=(t // tm, n // tn),
        in_specs=[pl.BlockSpec((tm, d), lambda i, j: (i, 0)), mspec, mspec,
                  pl.BlockSpec((1, d), lambda i, j: (0, 0)),
                  pl.BlockSpec((d, tn), lambda i, j: (0, j))],
        out_specs=[pl.BlockSpec((tm, tn), lambda i, j: (i, j)),
                   pl.BlockSpec((tm, d), lambda i, j: (i, 0))],
        out_shape=[jax.ShapeDtypeStruct((t, n), u_dtype), jax.ShapeDtypeStruct((t, d), BF16)],
        compiler_params=_cparams(("parallel", "arbitrary")),
        name="inproj_a",
    )(x, scale, shift, norm_pre, w)


def _inproj_b_kernel(h_ref, wk_ref, wv_ref, wg_ref, k_ref, v_ref, g_ref):
    h = h_ref[...]
    k_ref[...] = jnp.dot(h, wk_ref[...], preferred_element_type=F32)
    v_ref[...] = jnp.dot(h, wv_ref[...], preferred_element_type=F32)

    @pl.when(pl.program_id(1) == 0)
    def _():
        g_ref[...] = jnp.dot(h, wg_ref[...], preferred_element_type=F32)


def _inproj_b_call(h, wk, wv, wg, *, tm, tn):
    t, d = h.shape
    n = wk.shape[1]
    ng = wg.shape[1]
    return pl.pallas_call(
        _inproj_b_kernel,
        grid=(t // tm, n // tn),
        in_specs=[pl.BlockSpec((tm, d), lambda i, j: (i, 0)),
                  pl.BlockSpec((d, tn), lambda i, j: (0, j)),
                  pl.BlockSpec((d, tn), lambda i, j: (0, j)),
                  pl.BlockSpec((d, ng), lambda i, j: (0, 0))],
        out_specs=[pl.BlockSpec((tm, tn), lambda i, j: (i, j)),
                   pl.BlockSpec((tm, tn), lambda i, j: (i, j)),
                   pl.BlockSpec((tm, ng), lambda i, j: (i, 0))],
        out_shape=[jax.ShapeDtypeStruct((t, n), F32), jax.ShapeDtypeStruct((t, n), F32),
                   jax.ShapeDtypeStruct((t, ng), F32)],
        compiler_params=_cparams(("parallel", "arbitrary")),
        name="inproj_b",
    )(h, wk, wv, wg)


def _lambda_value(lamv_ref, lam_init):
    lv = lamv_ref[...]
    d1 = jnp.sum(lv[0:1] * lv[1:2], axis=-1, keepdims=True)
    d2 = jnp.sum(lv[2:3] * lv[3:4], axis=-1, keepdims=True)
    return jnp.exp(d1) - jnp.exp(d2) + lam_init


def _softmax_step(s, shift, v, m_ref, l_ref, acc_ref):
    m_old = m_ref[...]
    m_new = jnp.maximum(m_old, jnp.max(s, axis=-1, keepdims=True) + shift)
    alpha = jnp.exp(m_old - m_new)
    p = jnp.exp(s - (m_new - shift))
    l_ref[...] = alpha * l_ref[...] + jnp.sum(p, axis=-1, keepdims=True)
    acc_ref[...] = alpha * acc_ref[...] + jnp.dot(p.astype(BF16), v, preferred_element_type=F32)
    m_ref[...] = m_new


def _head_post(att, gain, lam_init, z):
    r = att * lax.rsqrt(jnp.mean(att * att, axis=-1, keepdims=True) + EPS) * gain
    return r * (1.0 - lam_init) * _silu(z)


def _attn_prompt_kernel(qi_ref, kj_ref, q_ref, k_ref, v_ref, z_ref, bias_ref, slope_ref, lamv_ref,
                        gn_ref, o_ref, qs_ref, m_ref, l_ref, acc_ref, *, blk, hd, lam_init):
    h = pl.program_id(1)
    p = pl.program_id(2)
    i = qi_ref[p]
    j = kj_ref[p]

    @pl.when(j == 0)
    def _():
        q = q_ref[...].astype(F32) * (hd ** -0.5)
        lane = lax.broadcasted_iota(jnp.int32, q.shape, 1)
        qs_ref[0:blk, :] = jnp.where(lane < hd, q, 0.0).astype(BF16)
        qs_ref[blk:2 * blk, :] = jnp.where(lane >= hd, q, 0.0).astype(BF16)
        m_ref[...] = jnp.full(m_ref.shape, NEG, F32)
        l_ref[...] = jnp.zeros(l_ref.shape, F32)
        acc_ref[...] = jnp.zeros(acc_ref.shape, F32)

    diag = (j == i).astype(jnp.int32)
    bias = bias_ref[diag]
    s = lax.dot_general(qs_ref[...], k_ref[...].astype(BF16), _NT, preferred_element_type=F32)
    s = s.reshape(2, blk, blk) + bias[None]
    shift = slope_ref[h] * ((j - i) * blk).astype(F32)
    _softmax_step(s.reshape(2 * blk, blk), shift, v_ref[...].astype(BF16), m_ref, l_ref, acc_ref)

    @pl.when(j == i)
    def _():
        o = acc_ref[...] / l_ref[...]
        lam = _lambda_value(lamv_ref, lam_init)
        att = o[0:blk] - lam * o[blk:2 * blk]
        o_ref[...] = _head_post(att, gn_ref[...], lam_init, z_ref[...].astype(F32)).astype(o_ref.dtype)


def _attn_prompt_call(u, k, v, lamv, gn, *, batch, seq, heads, hd, lam_init, z_col_blk):
    blk = ATT_BLOCK
    nb = seq // blk
    qi = np.array([i for i in range(nb) for _ in range(i + 1)], np.int32)
    kj = np.array([j for i in range(nb) for j in range(i + 1)], np.int32)
    slopes = 2.0 ** (-8.0 * np.arange(1, heads + 1, dtype=np.float32) / heads)
    rel = (np.arange(blk)[None, :] - np.arange(blk)[:, None]).astype(np.float32)
    off = slopes[:, None, None] * rel[None]
    dia = np.where(rel[None] <= 0, off, NEG).astype(np.float32)
    bias = jnp.asarray(np.stack([off, dia], axis=1))
    e = 2 * hd

    grid_spec = pltpu.PrefetchScalarGridSpec(
        num_scalar_prefetch=2,
        grid=(batch, heads, len(qi)),
        in_specs=[
            pl.BlockSpec((blk, e), lambda b, h, p, qi, kj: (b * nb + qi[p], h)),
            pl.BlockSpec((blk, e), lambda b, h, p, qi, kj: (b * nb + kj[p], h)),
            pl.BlockSpec((blk, e), lambda b, h, p, qi, kj: (b * nb + kj[p], h)),
            pl.BlockSpec((blk, e), lambda b, h, p, qi, kj: (b * nb + qi[p], z_col_blk + h)),
            pl.BlockSpec((None, 2, blk, blk), lambda b, h, p, qi, kj: (h, 0, 0, 0)),
            pl.BlockSpec(memory_space=pltpu.SMEM),
            pl.BlockSpec(lamv.shape, lambda b, h, p, qi, kj: (0, 0)),
            pl.BlockSpec((None, 1, e), lambda b, h, p, qi, kj: (h, 0, 0)),
        ],
        out_specs=pl.BlockSpec((blk, e), lambda b, h, p, qi, kj: (b * nb + qi[p], h)),
        scratch_shapes=[pltpu.VMEM((2 * blk, e), BF16), pltpu.VMEM((2 * blk, 1), F32),
                        pltpu.VMEM((2 * blk, 1), F32), pltpu.VMEM((2 * blk, e), F32)],
    )
    return pl.pallas_call(
        functools.partial(_attn_prompt_kernel, blk=blk, hd=hd, lam_init=lam_init),
        grid_spec=grid_spec,
        out_shape=jax.ShapeDtypeStruct((batch * seq, heads * e), BF16),
        compiler_params=_cparams(("parallel", "parallel", "arbitrary")),
        name="attn_prompt",
    )(jnp.asarray(qi), jnp.asarray(kj), u, k, v, u, bias, jnp.asarray(slopes), lamv, gn)


def _attn_sample_kernel(pt_ref, q_ref, z_ref, kn_ref, vn_ref, *rest, pps, heads, hd, tq, page, lam_init):
    kp_refs = rest[:pps]
    vp_refs = rest[pps:2 * pps]
    (bp_ref, bn_ref, slope_ref, lamv_ref, gn_ref, o_ref, qb_ref, m_ref, l_ref, acc_ref) = rest[2 * pps:]
    j = pl.program_id(1)
    e = 2 * hd
    nr = heads * tq

    @pl.when(j == 0)
    def _():
        q = q_ref[...].astype(F32) * (hd ** -0.5)
        lane = lax.broadcasted_iota(jnp.int32, (tq, e), 1)
        for hh in range(heads):
            qh = q[:, hh * e:(hh + 1) * e]
            qb_ref[hh * tq:(hh + 1) * tq, :] = jnp.where(lane < hd, qh, 0.0)
            qb_ref[nr + hh * tq:nr + (hh + 1) * tq, :] = jnp.where(lane >= hd, qh, 0.0)
        m_ref[...] = jnp.full(m_ref.shape, NEG, F32)
        l_ref[...] = jnp.zeros(l_ref.shape, F32)
        acc_ref[...] = jnp.zeros(acc_ref.shape, F32)

    qb = qb_ref[...].astype(BF16)
    for s in range(pps):
        kp = kp_refs[s][...].reshape(page * heads, e).astype(BF16)
        vp = vp_refs[s][...].reshape(page * heads, e).astype(BF16)
        sc = lax.dot_general(qb, kp, _NT, preferred_element_type=F32) + bp_ref[...]
        shift = slope_ref[...] * ((j * pps + s) * page).astype(F32)
        _softmax_step(sc, shift, vp, m_ref, l_ref, acc_ref)

    @pl.when(j == pl.num_programs(1) - 1)
    def _():
        kn = kn_ref[...]
        vn = vn_ref[...]
        knr = jnp.concatenate([kn[:, hh * e:(hh + 1) * e] for hh in range(heads)], axis=0).astype(BF16)
        vnr = jnp.concatenate([vn[:, hh * e:(hh + 1) * e] for hh in range(heads)], axis=0).astype(BF16)
        sc = lax.dot_general(qb, knr, _NT, preferred_element_type=F32) + bn_ref[...]
        _softmax_step(sc, 0.0, vnr, m_ref, l_ref, acc_ref)
        o = acc_ref[...] / l_ref[...]
        lam = _lambda_value(lamv_ref, lam_init)
        att = o[0:nr] - lam * o[nr:2 * nr]
        z = z_ref[...].astype(F32)
        for hh in range(heads):
            r = _head_post(att[hh * tq:(hh + 1) * tq], gn_ref[hh:hh + 1, :], lam_init,
                           z[:, hh * e:(hh + 1) * e])
            o_ref[:, hh * e:(hh + 1) * e] = r.astype(o_ref.dtype)


def _attn_sample_call(u, k_new, v_new, cache_k, cache_v, page_table, lamv, gn, *, tq, heads, hd,
                      lam_init, z_col_blk):
    bd, npg = page_table.shape
    page = cache_k.shape[1]
    pps = PAGES_PER_STEP
    e = 2 * hd
    nr = heads * tq
    past = npg * page
    slopes = 2.0 ** (-8.0 * np.arange(1, heads + 1, dtype=np.float32) / heads)
    r_head = (np.arange(2 * nr) % nr) // tq
    r_tok = np.arange(2 * nr) % tq
    r_slope = slopes[r_head]
    c_tok, c_head = np.arange(page * heads) // heads, np.arange(page * heads) % heads
    bp = r_slope[:, None] * (c_tok[None, :] - (past + r_tok)[:, None])
    bp = np.where(r_head[:, None] == c_head[None, :], bp, NEG).astype(np.float32)
    n_head, n_tok = np.arange(nr) // tq, np.arange(nr) % tq
    bn = r_slope[:, None] * (n_tok[None, :] - r_tok[:, None])
    ok = (r_head[:, None] == n_head[None, :]) & (n_tok[None, :] <= r_tok[:, None])
    bn = np.where(ok, bn, NEG).astype(np.float32)
    slope_col = r_slope.reshape(2 * nr, 1).astype(np.float32)
    w = heads * e

    def page_spec(s):
        return pl.BlockSpec((None, page, heads, e),
                            lambda b, j, pt: (pt[b * npg + j * pps + s], 0, 0, 0))

    const2 = lambda b, j, pt: (0, 0)
    grid_spec = pltpu.PrefetchScalarGridSpec(
        num_scalar_prefetch=1,
        grid=(bd, npg // pps),
        in_specs=[pl.BlockSpec((tq, w), lambda b, j, pt: (b, 0)),
                  pl.BlockSpec((tq, w), lambda b, j, pt: (b, z_col_blk)),
                  pl.BlockSpec((tq, w), lambda b, j, pt: (b, 0)),
                  pl.BlockSpec((tq, w), lambda b, j, pt: (b, 0))]
                 + [page_spec(s) for s in range(pps)] + [page_spec(s) for s in range(pps)]
                 + [pl.BlockSpec(bp.shape, const2), pl.BlockSpec(bn.shape, const2),
                    pl.BlockSpec(slope_col.shape, const2), pl.BlockSpec(lamv.shape, const2),
                    pl.BlockSpec(gn.shape, const2)],
        out_specs=pl.BlockSpec((tq, w), lambda b, j, pt: (b, 0)),
        scratch_shapes=[pltpu.VMEM((2 * nr, e), F32), pltpu.VMEM((2 * nr, 1), F32),
                        pltpu.VMEM((2 * nr, 1), F32), pltpu.VMEM((2 * nr, e), F32)],
    )
    return pl.pallas_call(
        functools.partial(_attn_sample_kernel, pps=pps, heads=heads, hd=hd, tq=tq, page=page,
                          lam_init=lam_init),
        grid_spec=grid_spec,
        out_shape=jax.ShapeDtypeStruct((bd * tq, w), F32),
        compiler_params=_cparams(("parallel", "arbitrary")),
        name="attn_sample",
    )(page_table.reshape(-1), u, u, k_new, v_new, *([cache_k] * pps), *([cache_v] * pps),
      jnp.asarray(bp), jnp.asarray(bn), jnp.asarray(slope_col), lamv, gn)


def _mlstm_chunk(q, k, v, ig_row, lf_row, c_st, n_st, m_st, scale):
    ln = q.shape[0]
    row = lax.broadcasted_iota(jnp.int32, (ln, ln), 0)
    col = lax.broadcasted_iota(jnp.int32, (ln, ln), 1)
    tri = col <= row
    eye = col == row
    lf_b = jnp.broadcast_to(lf_row, (ln, ln))
    ig_b = jnp.broadcast_to(ig_row, (ln, ln))
    b_col = jnp.sum(jnp.where(tri, lf_b, 0.0), axis=1, keepdims=True)
    lf_col = jnp.sum(jnp.where(eye, lf_b, 0.0), axis=1, keepdims=True)
    ig_col = jnp.sum(jnp.where(eye, ig_b, 0.0), axis=1, keepdims=True)
    b_row = jnp.sum(jnp.where(row <= col, jnp.broadcast_to(lf_col, (ln, ln)), 0.0),
                    axis=0, keepdims=True)
    dmat = jnp.where(tri, b_col - b_row + ig_row, NEG)
    a_col = b_col + m_st
    m_t = jnp.maximum(a_col, jnp.max(dmat, axis=1, keepdims=True))
    qk = lax.dot_general(q, k, _NT, preferred_element_type=F32) * scale
    s = qk * jnp.exp(dmat - m_t)
    inter = jnp.exp(a_col - m_t)
    q_c = lax.dot_general(q, c_st.astype(BF16), _NT, preferred_element_type=F32)
    num = inter * q_c + jnp.dot(s.astype(BF16), v, preferred_element_type=F32)
    qf = q.astype(F32)
    kf = k.astype(F32)
    den = inter * jnp.sum(qf * n_st, axis=1, keepdims=True) + jnp.sum(s, axis=1, keepdims=True)
    h = num / jnp.maximum(jnp.abs(den), jnp.exp(-m_t))
    m_new = m_t[ln - 1:ln, :]
    b_last = b_col[ln - 1:ln, :]
    wk = jnp.exp(b_last - b_col + ig_col - m_new)
    decay = jnp.exp(b_last + m_st - m_new)
    vw = (v.astype(F32) * wk).astype(BF16)
    c_new = decay * c_st + lax.dot_general(vw, k, _TN, preferred_element_type=F32) * scale
    n_new = decay * n_st + jnp.sum(kf * wk, axis=0, keepdims=True) * scale
    return h, c_new, n_new, m_new


def _log_sigmoid(x):
    return jnp.minimum(x, 0.0) - jnp.log(1.0 + jnp.exp(-jnp.abs(x)))


def _mlstm_post(h, gain, o, z):
    hn = h * lax.rsqrt(jnp.mean(h * h, axis=-1, keepdims=True) + EPS) * gain
    return _sigmoid(o) * hn * _silu(z)


def _mlstm_prompt_kernel(gb_ref, q_ref, k_ref, v_ref, o_ref, z_ref, g_ref, gn_ref, c0_ref, n0_ref, m0_ref,
                         y_ref, c_out, n_out, m_out, c_s, n_s, m_s, *, scale):
    hh = pl.program_id(1)
    ci = pl.program_id(2)

    @pl.when(ci == 0)
    def _():
        c_s[...] = c0_ref[...]
        n_s[...] = n0_ref[...]
        m_s[...] = m0_ref[...]

    g = g_ref[...]
    ig = g[0:1, :] + gb_ref[0, hh]
    lf = _log_sigmoid(g[1:2, :] + gb_ref[1, hh])
    h, c_new, n_new, m_new = _mlstm_chunk(q_ref[...], k_ref[...], v_ref[...], ig, lf,
                                          c_s[...], n_s[...], m_s[...], scale)
    c_s[...] = c_new
    n_s[...] = n_new
    m_s[...] = m_new
    y_ref[...] = _mlstm_post(h, gn_ref[...], o_ref[...].astype(F32), z_ref[...].astype(F32)).astype(y_ref.dtype)

    @pl.when(ci == pl.num_programs(2) - 1)
    def _():
        c_out[...] = c_new
        n_out[...] = n_new
        m_out[...] = m_new


def _mlstm_prompt_call(u, gates, gb, gn, c0, n0, m0, *, batch, seq, heads, dk, dv, cols):
    ln = MLSTM_CHUNK
    nc = seq // ln
    qb, kb, vb, ob, zb = (cols["q_m"] // dk, cols["k_m"] // dk, cols["v_m"] // dv, cols["o_m"] // dv,
                          cols["z_m"] // dv)
    row = lambda b, h, c: b * nc + c
    st4 = lambda b, h, c: (b, h, 0, 0)
    return pl.pallas_call(
        functools.partial(_mlstm_prompt_kernel, scale=dk ** -0.5),
        grid=(batch, heads, nc),
        in_specs=[pl.BlockSpec(memory_space=pltpu.SMEM),
                  pl.BlockSpec((ln, dk), lambda b, h, c: (row(b, h, c), qb + h)),
                  pl.BlockSpec((ln, dk), lambda b, h, c: (row(b, h, c), kb + h)),
                  pl.BlockSpec((ln, dv), lambda b, h, c: (row(b, h, c), vb + h)),
                  pl.BlockSpec((ln, dv), lambda b, h, c: (row(b, h, c), ob + h)),
                  pl.BlockSpec((ln, dv), lambda b, h, c: (row(b, h, c), zb + h)),
                  pl.BlockSpec((None, None, None, 2, ln), lambda b, h, c: (b, h, c, 0, 0)),
                  pl.BlockSpec((None, 1, dv), lambda b, h, c: (h, 0, 0)),
                  pl.BlockSpec((None, None, dv, dk), st4),
                  pl.BlockSpec((None, None, 1, dk), st4),
                  pl.BlockSpec((None, None, 1, 1), st4)],
        out_specs=[pl.BlockSpec((ln, dv), lambda b, h, c: (row(b, h, c), h)),
                   pl.BlockSpec((None, None, dv, dk), st4),
                   pl.BlockSpec((None, None, 1, dk), st4),
                   pl.BlockSpec((None, None, 1, 1), st4)],
        out_shape=[jax.ShapeDtypeStruct((batch * seq, heads * dv), BF16),
                   jax.ShapeDtypeStruct((batch, heads, dv, dk), F32),
                   jax.ShapeDtypeStruct((batch, heads, 1, dk), F32),
                   jax.ShapeDtypeStruct((batch, heads, 1, 1), F32)],
        scratch_shapes=[pltpu.VMEM((dv, dk), F32), pltpu.VMEM((1, dk), F32), pltpu.VMEM((1, 1), F32)],
        compiler_params=_cparams(("parallel", "parallel", "arbitrary")),
        name="mlstm_prompt",
    )(gb, u, u, u, u, u, gates, gn, c0, n0, m0)


def _mlstm_sample_kernel(gb_ref, q_ref, k_ref, v_ref, o_ref, z_ref, g_ref, gn_ref, c0_ref, n0_ref, m0_ref,
                         y_ref, c_out, n_out, m_out, *, scale, bb, tq):
    hh = pl.program_id(1)
    q = q_ref[...].astype(F32)
    k = k_ref[...].astype(F32)
    v = v_ref[...].astype(F32)
    o = o_ref[...].astype(F32)
    z = z_ref[...].astype(F32)
    gain = gn_ref[...]
    for i in range(bb):
        rs = slice(i * tq, (i + 1) * tq)
        g = g_ref[i]
        ig = g[0:1, :] + gb_ref[0, hh]
        lf = _log_sigmoid(g[1:2, :] + gb_ref[1, hh])
        h, c_new, n_new, m_new = _mlstm_chunk(q[rs].astype(BF16), k[rs].astype(BF16), v[rs].astype(BF16),
                                              ig, lf, c0_ref[i], n0_ref[i], m0_ref[i], scale)
        y_ref[rs, :] = _mlstm_post(h, gain, o[rs], z[rs]).astype(y_ref.dtype)
        c_out[i] = c_new
        n_out[i] = n_new
        m_out[i] = m_new


def _mlstm_sample_call(u, gates, gb, gn, c0, n0, m0, *, tq, heads, dk, dv, cols):
    bd = c0.shape[0]
    bb = MLSTM_DEC_BB
    qb, kb, vb, ob, zb = (cols["q_m"] // dk, cols["k_m"] // dk, cols["v_m"] // dv, cols["o_m"] // dv,
                          cols["z_m"] // dv)
    st4 = lambda g, h: (g, h, 0, 0)
    return pl.pallas_call(
        functools.partial(_mlstm_sample_kernel, scale=dk ** -0.5, bb=bb, tq=tq),
        grid=(bd // bb, heads),
        in_specs=[pl.BlockSpec(memory_space=pltpu.SMEM),
                  pl.BlockSpec((bb * tq, dk), lambda g, h: (g, qb + h)),
                  pl.BlockSpec((bb * tq, dk), lambda g, h: (g, kb + h)),
                  pl.BlockSpec((bb * tq, dv), lambda g, h: (g, vb + h)),
                  pl.BlockSpec((bb * tq, dv), lambda g, h: (g, ob + h)),
                  pl.BlockSpec((bb * tq, dv), lambda g, h: (g, zb + h)),
                  pl.BlockSpec((bb, None, 2, tq), st4),
                  pl.BlockSpec((None, 1, dv), lambda g, h: (h, 0, 0)),
                  pl.BlockSpec((bb, None, dv, dk), st4),
                  pl.BlockSpec((bb, None, 1, dk), st4),
                  pl.BlockSpec((bb, None, 1, 1), st4)],
        out_specs=[pl.BlockSpec((bb * tq, dv), lambda g, h: (g, h)),
                   pl.BlockSpec((bb, None, dv, dk), st4),
                   pl.BlockSpec((bb, None, 1, dk), st4),
                   pl.BlockSpec((bb, None, 1, 1), st4)],
        out_shape=[jax.ShapeDtypeStruct((bd * tq, heads * dv), F32),
                   jax.ShapeDtypeStruct((bd, heads, dv, dk), F32),
                   jax.ShapeDtypeStruct((bd, heads, 1, dk), F32),
                   jax.ShapeDtypeStruct((bd, heads, 1, 1), F32)],
        compiler_params=_cparams(("parallel", "parallel")),
        name="mlstm_sample",
    )(gb, u, u, u, u, u, gates, gn, c0, n0, m0)


def _out_kernel(x_ref, ya_ref, ym_ref, ga_ref, gm_ref, gate_ref, np_ref, wa_ref, wm_ref, wo_ref, o_ref,
                *, rowwise):
    ya = jnp.dot(ya_ref[...].astype(BF16), wa_ref[...], preferred_element_type=F32)
    ym = jnp.dot(ym_ref[...].astype(BF16), wm_ref[...], preferred_element_type=F32)
    y2 = _sigmoid(ga_ref[...].astype(F32)) * ya + _sigmoid(gm_ref[...].astype(F32)) * ym
    y = jnp.dot(y2.astype(BF16), wo_ref[...], preferred_element_type=F32)
    yn = y * lax.rsqrt(jnp.mean(y * y, axis=-1, keepdims=True) + EPS) * np_ref[...]
    gate = gate_ref[...] if rowwise else gate_ref[0]
    o_ref[...] = x_ref[...] + gate * yn


def _out_call(x, ya, ym, u, gate, norm_post, wa, wm, wo, *, rowwise, rows_per_group, tm, cols):
    t, d = x.shape
    da = ya.shape[1]
    ga_blk, gm_blk = cols["g_a"] // d, cols["g_m"] // d
    if rowwise:
        gspec = pl.BlockSpec((tm, d), lambda i: (i, 0))
    else:
        gspec = pl.BlockSpec((1, 1, d), lambda i: ((i * tm) // rows_per_group, 0, 0))
    resident = dict(pipeline_mode=pl.Buffered(1))
    return pl.pallas_call(
        functools.partial(_out_kernel, rowwise=rowwise),
        grid=(t // tm,),
        in_specs=[pl.BlockSpec((tm, d), lambda i: (i, 0)),
                  pl.BlockSpec((tm, da), lambda i: (i, 0)),
                  pl.BlockSpec((tm, da), lambda i: (i, 0)),
                  pl.BlockSpec((tm, d), lambda i: (i, ga_blk)),
                  pl.BlockSpec((tm, d), lambda i: (i, gm_blk)),
                  gspec,
                  pl.BlockSpec((1, d), lambda i: (0, 0)),
                  pl.BlockSpec(wa.shape, lambda i: (0, 0), **resident),
                  pl.BlockSpec(wm.shape, lambda i: (0, 0), **resident),
                  pl.BlockSpec(wo.shape, lambda i: (0, 0), **resident)],
        out_specs=pl.BlockSpec((tm, d), lambda i: (i, 0)),
        out_shape=jax.ShapeDtypeStruct((t, d), F32),
        compiler_params=_cparams(("parallel",)),
        name="out_stage",
    )(x, ya, ym, u, u, gate, norm_post, wa, wm, wo)


def kernel(x_prompt, x_sample, cache_k, cache_v, state_C, state_n, state_m, page_table, c_prompt, c_sample,
           w_ada, b_ada, norm_pre, norm_post, w_in, b_igate, b_fgate, lambda_q1, lambda_k1, lambda_q2,
           lambda_k2, attn_head_norm, mlstm_head_norm, w_br_a, w_br_m, w_out):
    depth = w_in.shape[0]
    assert depth == 1, "single-layer trunk"
    batch, seq, d = x_prompt.shape
    bd, tq, _ = x_sample.shape
    heads_a, e = cache_k.shape[3], cache_k.shape[4]
    hd = e // 2
    d_att = heads_a * e
    heads_m, dv, dk = state_C.shape[2], state_C.shape[3], state_C.shape[4]
    d_mv, d_mqk = heads_m * dv, heads_m * dk
    layer = 0
    lam_init = 0.8 - 0.6 * math.exp(-0.3 * layer)

    names = ("q_a", "k_a", "v_a", "z_a", "q_m", "k_m", "v_m", "o_m", "z_m", "i_m", "f_m", "g_a", "g_m")
    widths = (d_att, d_att, d_att, d_att, d_mqk, d_mqk, d_mv, d_mv, d_mv, heads_m, heads_m, d, d)
    src = dict(zip(names, np.concatenate([[0], np.cumsum(widths)[:-1]]).tolist()))
    wid = dict(zip(names, widths))
    packed = ("q_a", "z_a", "q_m", "k_m", "v_m", "o_m", "z_m", "g_a", "g_m")
    cols, off = {}, 0
    for nm in packed:
        cols[nm] = off
        off += wid[nm]

    w = w_in[layer]
    seg = lambda nm: w[:, src[nm]:src[nm] + wid[nm]]
    w_pack = jnp.concatenate([seg(nm) for nm in packed], axis=1).astype(BF16)
    w_k = seg("k_a").astype(BF16)
    w_v = seg("v_a").astype(BF16)
    w_g = jnp.concatenate([seg("i_m"), seg("f_m"), jnp.zeros((d, LANE - 2 * heads_m), F32)], axis=1).astype(BF16)
    wa = w_br_a[layer].astype(BF16)
    wm = w_br_m[layer].astype(BF16)
    wo = w_out[layer].astype(BF16)

    n_c = batch + bd
    pad = (-n_c) % 8
    c_all = jnp.concatenate([c_prompt, c_sample, jnp.zeros((pad, d), F32)], axis=0)
    mod = _mod_call(c_all, w_ada[layer], b_ada[layer].reshape(1, -1))
    shift, scale, gate = mod[:, :d], mod[:, d:2 * d], mod[:, 2 * d:]

    lamv = jnp.stack([lambda_q1[layer], lambda_k1[layer], lambda_q2[layer], lambda_k2[layer]])
    gn_a = attn_head_norm[layer].reshape(heads_a, e)
    gn_m = mlstm_head_norm[layer].reshape(heads_m, 1, dv)
    gb = jnp.stack([b_igate[layer], b_fgate[layer]])
    npre = norm_pre[layer].reshape(1, d)
    npost = norm_post[layer].reshape(1, d)

    xp = x_prompt.reshape(batch * seq, d)
    grp = lambda a: a[:batch].reshape(batch, 1, d)
    u_p, h_p = _inproj_a_call(xp, grp(scale), grp(shift), npre, w_pack, rowwise=False,
                              rows_per_group=seq, tm=1024, tn=1024, u_dtype=BF16)
    k_p, v_p, g_p = _inproj_b_call(h_p, w_k, w_v, w_g, tm=1024, tn=512)
    ya_p = _attn_prompt_call(u_p, k_p, v_p, lamv, gn_a.reshape(heads_a, 1, e), batch=batch, seq=seq,
                             heads=heads_a, hd=hd, lam_init=lam_init, z_col_blk=cols["z_a"] // e)
    nc = seq // MLSTM_CHUNK
    gates_p = g_p[:, :2 * heads_m].reshape(batch, nc, MLSTM_CHUNK, 2, heads_m).transpose(0, 4, 1, 3, 2)
    ym_p, c_p, n_p, m_p = _mlstm_prompt_call(
        u_p, gates_p, gb, gn_m, jnp.zeros((batch, heads_m, dv, dk), F32), jnp.zeros((batch, heads_m, 1, dk), F32),
        jnp.zeros((batch, heads_m, 1, 1), F32), batch=batch, seq=seq, heads=heads_m, dk=dk, dv=dv, cols=cols)
    y_p = _out_call(xp, ya_p, ym_p, u_p, grp(gate), npost, wa, wm, wo, rowwise=False, rows_per_group=seq,
                    tm=512, cols=cols)

    xs = x_sample.reshape(bd * tq, d)
    rep = lambda a: jnp.repeat(a[batch:batch + bd], tq, axis=0)
    u_s, h_s = _inproj_a_call(xs, rep(scale), rep(shift), npre, w_pack, rowwise=True, rows_per_group=tq,
                              tm=512, tn=1024, u_dtype=F32)
    k_s, v_s, g_s = _inproj_b_call(h_s, w_k, w_v, w_g, tm=512, tn=512)
    ya_s = _attn_sample_call(u_s, k_s, v_s, cache_k.reshape(cache_k.shape[1:]),
                             cache_v.reshape(cache_v.shape[1:]), page_table, lamv, gn_a,
                             tq=tq, heads=heads_a, hd=hd, lam_init=lam_init, z_col_blk=cols["z_a"] // d_att)
    gates_s = g_s[:, :2 * heads_m].reshape(bd, tq, 2, heads_m).transpose(0, 3, 2, 1)
    ym_s, c_s, n_s, m_s = _mlstm_sample_call(
        u_s, gates_s, gb, gn_m, state_C.reshape(state_C.shape[1:]), state_n.reshape(bd, heads_m, 1, dk),
        state_m.reshape(bd, heads_m, 1, 1), tq=tq, heads=heads_m, dk=dk, dv=dv, cols=cols)
    y_s = _out_call(xs, ya_s, ym_s, u_s, rep(gate), npost, wa, wm, wo, rowwise=True, rows_per_group=tq,
                    tm=256, cols=cols)

    return (y_p.reshape(batch, seq, d), y_s.reshape(bd, tq, d),
            k_p.reshape(1, batch, seq, heads_a, e), v_p.reshape(1, batch, seq, heads_a, e),
            c_p.reshape(1, batch, heads_m, dv, dk), n_p.reshape(1, batch, heads_m, dk),
            m_p.reshape(1, batch, heads_m),
            k_s.reshape(1, bd, tq, heads_a, e), v_s.reshape(1, bd, tq, heads_a, e),
            c_s.reshape(1, bd, heads_m, dv, dk), n_s.reshape(1, bd, heads_m, dk), m_s.reshape(1, bd, heads_m))
```

```python
import functools
import math

import jax
import jax.numpy as jnp
import numpy as np
from jax import lax
from jax.experimental import pallas as pl
from jax.experimental.pallas import tpu as pltpu

F32 = jnp.float32
BF16 = jnp.bfloat16
EPS = 1e-6
NEG = -1e30
VMEM_LIMIT = 56 * 1024 * 1024
LANE = 128
MLSTM_CHUNK = 256
ATT_BLOCK = 512
PAGES_PER_STEP = 16
MLSTM_DEC_BB = 8

_NT = (((1,), (1,)), ((), ()))
_TN = (((0,), (0,)), ((), ()))


def _cparams(sem):
    return pltpu.CompilerParams(dimension_semantics=sem, vmem_limit_bytes=VMEM_LIMIT)


def _sigmoid(x):
    return 1.0 / (1.0 + jnp.exp(-x))


def _silu(x):
    return x * _sigmoid(x)


def _mod_kernel(c_ref, w_ref, b_ref, o_ref):
    a = _silu(c_ref[...]).astype(BF16)
    o_ref[...] = jnp.dot(a, w_ref[...].astype(BF16), preferred_element_type=F32) + b_ref[...]


def _mod_call(c_all, w_ada, b_ada):
    rows, d = c_all.shape
    n = w_ada.shape[1]
    tn = 512
    return pl.pallas_call(
        _mod_kernel,
        grid=(n // tn,),
        in_specs=[pl.BlockSpec((rows, d), lambda j: (0, 0)),
                  pl.BlockSpec((d, tn), lambda j: (0, j)),
                  pl.BlockSpec((1, tn), lambda j: (0, j))],
        out_specs=pl.BlockSpec((rows, tn), lambda j: (0, j)),
        out_shape=jax.ShapeDtypeStruct((rows, n), F32),
        compiler_params=_cparams(("parallel",)),
        name="mod",
    )(c_all, w_ada, b_ada)


def _inproj_kernel(x_ref, sc_ref, sh_ref, g_ref, wm_ref, wt_ref, wg_ref, u_ref, k_ref, v_ref, gate_ref, *rest,
                   rowwise, n_main, kj0, vj0, nkv, want_vt):
    if want_vt:
        vt_ref, h_ref = rest
    else:
        (h_ref,) = rest
    j = pl.program_id(1)

    @pl.when(j == 0)
    def _():
        x = x_ref[...]
        xn = x * lax.rsqrt(jnp.mean(x * x, axis=-1, keepdims=True) + EPS) * g_ref[...]
        sc = sc_ref[...] if rowwise else sc_ref[0]
        sh = sh_ref[...] if rowwise else sh_ref[0]
        h = (xn * (1.0 + sc) + sh).astype(BF16)
        h_ref[...] = h
        gate_ref[...] = jnp.dot(h, wg_ref[...], preferred_element_type=F32)

    @pl.when(j < n_main)
    def _():
        res = jnp.dot(h_ref[...], wm_ref[...].astype(BF16), preferred_element_type=F32)
        u_ref[...] = res.astype(u_ref.dtype)

        @pl.when((j >= kj0) & (j < kj0 + nkv))
        def _():
            k_ref[...] = res

        @pl.when((j >= vj0) & (j < vj0 + nkv))
        def _():
            v_ref[...] = res
            if want_vt:
                vt_ref[...] = res.T.astype(BF16)

    @pl.when(j >= n_main)
    def _():
        u_ref[...] = jnp.dot(h_ref[...], wt_ref[...], preferred_element_type=F32).astype(u_ref.dtype)


def _inproj_call(x, scale, shift, norm_pre, w_full, w_tail, w_g, *, rowwise, rows_per_group, tm, tn, u_dtype,
                 main_cols, k_col, v_col, kv_cols, want_vt, x_buffers):
    t, d = x.shape
    n_main, n_tail = main_cols // tn, w_tail.shape[1] // tn
    kj0, vj0, nkv = k_col // tn, v_col // tn, kv_cols // tn
    ng = w_g.shape[1]
    clamp = lambda j, j0: jnp.clip(j - j0, 0, nkv - 1)
    if rowwise:
        mspec = pl.BlockSpec((tm, d), lambda i, j: (i, 0))
    else:
        mspec = pl.BlockSpec((1, 1, d), lambda i, j: ((i * tm) // rows_per_group, 0, 0))
    out_specs = [pl.BlockSpec((tm, tn), lambda i, j: (i, j)),
                 pl.BlockSpec((tm, tn), lambda i, j: (i, clamp(j, kj0))),
                 pl.BlockSpec((tm, tn), lambda i, j: (i, clamp(j, vj0))),
                 pl.BlockSpec((tm, ng), lambda i, j: (i, 0))]
    out_shape = [jax.ShapeDtypeStruct((t, (n_main + n_tail) * tn), u_dtype),
                 jax.ShapeDtypeStruct((t, kv_cols), F32), jax.ShapeDtypeStruct((t, kv_cols), F32),
                 jax.ShapeDtypeStruct((t, ng), F32)]
    if want_vt:
        out_specs.append(pl.BlockSpec((tn, tm), lambda i, j: (clamp(j, vj0), i)))
        out_shape.append(jax.ShapeDtypeStruct((kv_cols, t), BF16))
    return pl.pallas_call(
        functools.partial(_inproj_kernel, rowwise=rowwise, n_main=n_main, kj0=kj0, vj0=vj0, nkv=nkv,
                          want_vt=want_vt),
        grid=(t // tm, n_main + n_tail),
        in_specs=[pl.BlockSpec((tm, d), lambda i, j: (i, 0), pipeline_mode=pl.Buffered(x_buffers)),
                  mspec, mspec,
                  pl.BlockSpec((1, d), lambda i, j: (0, 0)),
                  pl.BlockSpec((d, tn), lambda i, j: (0, jnp.minimum(j, n_main - 1))),
                  pl.BlockSpec((d, tn), lambda i, j: (0, jnp.maximum(j - n_main, 0))),
                  pl.BlockSpec((d, ng), lambda i, j: (0, 0))],
        out_specs=out_specs,
        out_shape=out_shape,
        scratch_shapes=[pltpu.VMEM((tm, d), BF16)],
        compiler_params=_cparams(("parallel", "arbitrary")),
        name="inproj",
    )(x, scale, shift, norm_pre, w_full, w_tail, w_g)


def _lambda_value(lamv_ref, lam_init):
    lv = lamv_ref[...]
    d1 = jnp.sum(lv[0:1] * lv[1:2], axis=-1, keepdims=True)
    d2 = jnp.sum(lv[2:3] * lv[3:4], axis=-1, keepdims=True)
    return jnp.exp(d1) - jnp.exp(d2) + lam_init


def _softmax_step(s, shift, v, m_ref, l_ref, acc_ref):
    m_old = m_ref[...]
    m_new = jnp.maximum(m_old, jnp.max(s, axis=-1, keepdims=True) + shift)
    alpha = jnp.exp(m_old - m_new)
    p = jnp.exp(s - (m_new - shift))
    l_ref[...] = alpha * l_ref[...] + jnp.sum(p, axis=-1, keepdims=True)
    acc_ref[...] = alpha * acc_ref[...] + jnp.dot(p.astype(BF16), v, preferred_element_type=F32)
    m_ref[...] = m_new


def _head_post(att, gain, lam_init, z):
    r = att * lax.rsqrt(jnp.mean(att * att, axis=-1, keepdims=True) + EPS) * gain
    return r * (1.0 - lam_init) * _silu(z)


def _attn_prompt_kernel(qi_ref, kj_ref, q_ref, k_ref, vt_ref, z_ref, bias_ref, slope_ref, lamv_ref,
                        gn_ref, o_ref, qs_ref, m_ref, l_ref, acc_ref, *, blk, hd, lam_init):
    h = pl.program_id(1)
    p = pl.program_id(2)
    i = qi_ref[p]
    j = kj_ref[p]

    @pl.when(j == 0)
    def _():
        q = q_ref[...].astype(F32) * (hd ** -0.5)
        lane = lax.broadcasted_iota(jnp.int32, q.shape, 1)
        qs_ref[0] = jnp.where(lane < hd, q, 0.0).astype(BF16)
        qs_ref[1] = jnp.where(lane >= hd, q, 0.0).astype(BF16)
        m_ref[...] = jnp.full(m_ref.shape, NEG, F32)
        l_ref[...] = jnp.zeros(l_ref.shape, F32)
        acc_ref[...] = jnp.zeros(acc_ref.shape, F32)

    diag = (j == i).astype(jnp.int32)
    bias = bias_ref[diag]
    shift = slope_ref[h] * ((j - i) * blk).astype(F32)
    k = k_ref[...]
    vt = vt_ref[...]
    for c in range(2):
        s = lax.dot_general(k, qs_ref[c], _NT, preferred_element_type=F32) + bias
        m_old = m_ref[c]
        m_new = jnp.maximum(m_old, jnp.max(s, axis=0, keepdims=True) + shift)
        alpha = jnp.exp(m_old - m_new)
        pr = jnp.exp(s - (m_new - shift))
        l_ref[c] = alpha * l_ref[c] + jnp.sum(pr, axis=0, keepdims=True)
        acc_ref[c] = alpha * acc_ref[c] + jnp.dot(vt, pr.astype(BF16), preferred_element_type=F32)
        m_ref[c] = m_new

    @pl.when(j == i)
    def _():
        lam = _lambda_value(lamv_ref, lam_init)
        att = acc_ref[0] / l_ref[0] - lam * (acc_ref[1] / l_ref[1])
        r = att * lax.rsqrt(jnp.mean(att * att, axis=0, keepdims=True) + EPS) * gn_ref[...]
        o_ref[...] = (r.T * (1.0 - lam_init) * _silu(z_ref[...].astype(F32))).astype(o_ref.dtype)


def _attn_prompt_call(u, vt, lamv, gn, *, batch, seq, heads, hd, lam_init, cols):
    blk = ATT_BLOCK
    nb = seq // blk
    qi = np.array([i for i in range(nb) for _ in range(i + 1)], np.int32)
    kj = np.array([j for i in range(nb) for j in range(i + 1)], np.int32)
    slopes = 2.0 ** (-8.0 * np.arange(1, heads + 1, dtype=np.float32) / heads)
    rel = (np.arange(blk)[:, None] - np.arange(blk)[None, :]).astype(np.float32)
    off = slopes[:, None, None] * rel[None]
    dia = np.where(rel[None] <= 0, off, NEG).astype(np.float32)
    bias = jnp.asarray(np.stack([off, dia], axis=1))
    e = 2 * hd
    qb, kb, zb = cols["q_a"] // e, cols["k_a"] // e, cols["z_a"] // e

    grid_spec = pltpu.PrefetchScalarGridSpec(
        num_scalar_prefetch=2,
        grid=(batch, heads, len(qi)),
        in_specs=[
            pl.BlockSpec((blk, e), lambda b, h, p, qi, kj: (b * nb + qi[p], qb + h)),
            pl.BlockSpec((blk, e), lambda b, h, p, qi, kj: (b * nb + kj[p], kb + h)),
            pl.BlockSpec((e, blk), lambda b, h, p, qi, kj: (h, b * nb + kj[p])),
            pl.BlockSpec((blk, e), lambda b, h, p, qi, kj: (b * nb + qi[p], zb + h)),
            pl.BlockSpec((None, 2, blk, blk), lambda b, h, p, qi, kj: (h, 0, 0, 0)),
            pl.BlockSpec(memory_space=pltpu.SMEM),
            pl.BlockSpec(lamv.shape, lambda b, h, p, qi, kj: (0, 0)),
            pl.BlockSpec((None, e, 1), lambda b, h, p, qi, kj: (h, 0, 0)),
        ],
        out_specs=pl.BlockSpec((blk, e), lambda b, h, p, qi, kj: (b * nb + qi[p], h)),
        scratch_shapes=[pltpu.VMEM((2, blk, e), BF16), pltpu.VMEM((2, 1, blk), F32),
                        pltpu.VMEM((2, 1, blk), F32), pltpu.VMEM((2, e, blk), F32)],
    )
    return pl.pallas_call(
        functools.partial(_attn_prompt_kernel, blk=blk, hd=hd, lam_init=lam_init),
        grid_spec=grid_spec,
        out_shape=jax.ShapeDtypeStruct((batch * seq, heads * e), BF16),
        compiler_params=_cparams(("parallel", "parallel", "arbitrary")),
        name="attn_prompt",
    )(jnp.asarray(qi), jnp.asarray(kj), u, u, vt, u, bias, jnp.asarray(slopes), lamv, gn)


def _attn_sample_kernel(pt_ref, q_ref, z_ref, kn_ref, vn_ref, *rest, pps, heads, hd, tq, page, lam_init):
    kp_refs = rest[:pps]
    vp_refs = rest[pps:2 * pps]
    (bp_ref, bn_ref, slope_ref, lamv_ref, gn_ref, o_ref, qb_ref, s_ref, m_ref, l_ref, acc_ref) = rest[2 * pps:]
    j = pl.program_id(1)
    e = 2 * hd
    nr = heads * tq
    pw = page * heads

    @pl.when(j == 0)
    def _():
        q = q_ref[...].astype(F32) * (hd ** -0.5)
        lane = lax.broadcasted_iota(jnp.int32, (tq, e), 1)
        for hh in range(heads):
            qh = q[:, hh * e:(hh + 1) * e]
            qb_ref[hh * tq:(hh + 1) * tq, :] = jnp.where(lane < hd, qh, 0.0)
            qb_ref[nr + hh * tq:nr + (hh + 1) * tq, :] = jnp.where(lane >= hd, qh, 0.0)
        m_ref[...] = jnp.full(m_ref.shape, NEG, F32)
        l_ref[...] = jnp.zeros(l_ref.shape, F32)
        acc_ref[...] = jnp.zeros(acc_ref.shape, F32)

    qb = qb_ref[...].astype(BF16)
    shifts = [slope_ref[...] * ((j * pps + s) * page).astype(F32) for s in range(pps)]
    mx = None
    for s in range(pps):
        kp = kp_refs[s][...].reshape(pw, e).astype(BF16)
        sc = lax.dot_general(qb, kp, _NT, preferred_element_type=F32) + bp_ref[...]
        s_ref[:, s * pw:(s + 1) * pw] = sc
        ms = jnp.max(sc, axis=-1, keepdims=True) + shifts[s]
        mx = ms if mx is None else jnp.maximum(mx, ms)
    m_old = m_ref[...]
    m_new = jnp.maximum(m_old, mx)
    alpha = jnp.exp(m_old - m_new)
    lsum = jnp.zeros_like(m_old)
    pv = jnp.zeros(acc_ref.shape, F32)
    for s in range(pps):
        pr = jnp.exp(s_ref[:, s * pw:(s + 1) * pw] - (m_new - shifts[s]))
        lsum = lsum + jnp.sum(pr, axis=-1, keepdims=True)
        vp = vp_refs[s][...].reshape(pw, e).astype(BF16)
        pv = pv + jnp.dot(pr.astype(BF16), vp, preferred_element_type=F32)
    l_ref[...] = alpha * l_ref[...] + lsum
    acc_ref[...] = alpha * acc_ref[...] + pv
    m_ref[...] = m_new

    @pl.when(j == pl.num_programs(1) - 1)
    def _():
        kn = kn_ref[...]
        vn = vn_ref[...]
        knr = jnp.concatenate([kn[:, hh * e:(hh + 1) * e] for hh in range(heads)], axis=0).astype(BF16)
        vnr = jnp.concatenate([vn[:, hh * e:(hh + 1) * e] for hh in range(heads)], axis=0).astype(BF16)
        sc = lax.dot_general(qb, knr, _NT, preferred_element_type=F32) + bn_ref[...]
        _softmax_step(sc, 0.0, vnr, m_ref, l_ref, acc_ref)
        o = acc_ref[...] / l_ref[...]
        lam = _lambda_value(lamv_ref, lam_init)
        att = o[0:nr] - lam * o[nr:2 * nr]
        z = z_ref[...].astype(F32)
        for hh in range(heads):
            r = _head_post(att[hh * tq:(hh + 1) * tq], gn_ref[hh:hh + 1, :], lam_init,
                           z[:, hh * e:(hh + 1) * e])
            o_ref[:, hh * e:(hh + 1) * e] = r.astype(o_ref.dtype)


def _attn_sample_call(u, k_new, v_new, cache_k, cache_v, page_table, lamv, gn, *, tq, heads, hd,
                      lam_init, cols):
    bd, npg = page_table.shape
    page = cache_k.shape[1]
    pps = PAGES_PER_STEP
    e = 2 * hd
    nr = heads * tq
    past = npg * page
    slopes = 2.0 ** (-8.0 * np.arange(1, heads + 1, dtype=np.float32) / heads)
    r_head = (np.arange(2 * nr) % nr) // tq
    r_tok = np.arange(2 * nr) % tq
    r_slope = slopes[r_head]
    c_tok, c_head = np.arange(page * heads) // heads, np.arange(page * heads) % heads
    bp = r_slope[:, None] * (c_tok[None, :] - (past + r_tok)[:, None])
    bp = np.where(r_head[:, None] == c_head[None, :], bp, NEG).astype(np.float32)
    n_head, n_tok = np.arange(nr) // tq, np.arange(nr) % tq
    bn = r_slope[:, None] * (n_tok[None, :] - r_tok[:, None])
    ok = (r_head[:, None] == n_head[None, :]) & (n_tok[None, :] <= r_tok[:, None])
    bn = np.where(ok, bn, NEG).astype(np.float32)
    slope_col = r_slope.reshape(2 * nr, 1).astype(np.float32)
    w = heads * e

    def page_spec(s):
        return pl.BlockSpec((None, page, heads, e),
                            lambda b, j, pt: (pt[b * npg + j * pps + s], 0, 0, 0))

    const2 = lambda b, j, pt: (0, 0)
    grid_spec = pltpu.PrefetchScalarGridSpec(
        num_scalar_prefetch=1,
        grid=(bd, npg // pps),
        in_specs=[pl.BlockSpec((tq, w), lambda b, j, pt: (b, cols["q_a"] // w)),
                  pl.BlockSpec((tq, w), lambda b, j, pt: (b, cols["z_a"] // w)),
                  pl.BlockSpec((tq, w), lambda b, j, pt: (b, 0)),
                  pl.BlockSpec((tq, w), lambda b, j, pt: (b, 0))]
                 + [page_spec(s) for s in range(pps)] + [page_spec(s) for s in range(pps)]
                 + [pl.BlockSpec(bp.shape, const2), pl.BlockSpec(bn.shape, const2),
                    pl.BlockSpec(slope_col.shape, const2), pl.BlockSpec(lamv.shape, const2),
                    pl.BlockSpec(gn.shape, const2)],
        out_specs=pl.BlockSpec((tq, w), lambda b, j, pt: (b, 0)),
        scratch_shapes=[pltpu.VMEM((2 * nr, e), F32), pltpu.VMEM((2 * nr, pps * page * heads), F32),
                        pltpu.VMEM((2 * nr, 1), F32), pltpu.VMEM((2 * nr, 1), F32),
                        pltpu.VMEM((2 * nr, e), F32)],
    )
    return pl.pallas_call(
        functools.partial(_attn_sample_kernel, pps=pps, heads=heads, hd=hd, tq=tq, page=page,
                          lam_init=lam_init),
        grid_spec=grid_spec,
        out_shape=jax.ShapeDtypeStruct((bd * tq, w), F32),
        compiler_params=_cparams(("parallel", "arbitrary")),
        name="attn_sample",
    )(page_table.reshape(-1), u, u, k_new, v_new, *([cache_k] * pps), *([cache_v] * pps),
      jnp.asarray(bp), jnp.asarray(bn), jnp.asarray(slope_col), lamv, gn)


def _mlstm_chunk(q, k, v, ig_row, lf_row, c_st, n_st, m_st, scale):
    ln = q.shape[0]
    row = lax.broadcasted_iota(jnp.int32, (ln, ln), 0)
    col = lax.broadcasted_iota(jnp.int32, (ln, ln), 1)
    tri = col <= row
    eye = col == row
    lf_b = jnp.broadcast_to(lf_row, (ln, ln))
    ig_b = jnp.broadcast_to(ig_row, (ln, ln))
    b_col = jnp.sum(jnp.where(tri, lf_b, 0.0), axis=1, keepdims=True)
    lf_col = jnp.sum(jnp.where(eye, lf_b, 0.0), axis=1, keepdims=True)
    ig_col = jnp.sum(jnp.where(eye, ig_b, 0.0), axis=1, keepdims=True)
    b_row = jnp.sum(jnp.where(row <= col, jnp.broadcast_to(lf_col, (ln, ln)), 0.0),
                    axis=0, keepdims=True)
    dmat = jnp.where(tri, b_col - b_row + ig_row, NEG)
    a_col = b_col + m_st
    m_t = jnp.maximum(a_col, jnp.max(dmat, axis=1, keepdims=True))
    qk = lax.dot_general(q, k, _NT, preferred_element_type=F32) * scale
    s = qk * jnp.exp(dmat - m_t)
    inter = jnp.exp(a_col - m_t)
    q_c = lax.dot_general(q, c_st.astype(BF16), _NT, preferred_element_type=F32)
    num = inter * q_c + jnp.dot(s.astype(BF16), v, preferred_element_type=F32)
    qf = q.astype(F32)
    kf = k.astype(F32)
    den = inter * jnp.sum(qf * n_st, axis=1, keepdims=True) + jnp.sum(s, axis=1, keepdims=True)
    h = num / jnp.maximum(jnp.abs(den), jnp.exp(-m_t))
    m_new = m_t[ln - 1:ln, :]
    b_last = b_col[ln - 1:ln, :]
    wk = jnp.exp(b_last - b_col + ig_col - m_new)
    decay = jnp.exp(b_last + m_st - m_new)
    vw = (v.astype(F32) * wk).astype(BF16)
    c_new = decay * c_st + lax.dot_general(vw, k, _TN, preferred_element_type=F32) * scale
    n_new = decay * n_st + jnp.sum(kf * wk, axis=0, keepdims=True) * scale
    return h, c_new, n_new, m_new


def _log_sigmoid(x):
    return jnp.minimum(x, 0.0) - jnp.log(1.0 + jnp.exp(-jnp.abs(x)))


def _mlstm_post(h, gain, o, z):
    hn = h * lax.rsqrt(jnp.mean(h * h, axis=-1, keepdims=True) + EPS) * gain
    return _sigmoid(o) * hn * _silu(z)


def _mlstm_prompt_kernel(gb_ref, q_ref, k_ref, v_ref, o_ref, z_ref, g_ref, gn_ref, c0_ref, n0_ref, m0_ref,
                         y_ref, c_out, n_out, m_out, c_s, n_s, m_s, *, scale, heads, dk, dv):
    ci = pl.program_id(1)

    @pl.when(ci == 0)
    def _():
        c_s[...] = c0_ref[...]
        n_s[...] = n0_ref[...]
        m_s[...] = m0_ref[...]

    g = g_ref[...]
    for hh in range(heads):
        ks = slice(hh * dk, (hh + 1) * dk)
        vs = slice(hh * dv, (hh + 1) * dv)
        ig = g[hh:hh + 1, :] + gb_ref[0, hh]
        lf = _log_sigmoid(g[heads + hh:heads + hh + 1, :] + gb_ref[1, hh])
        h, c_new, n_new, m_new = _mlstm_chunk(q_ref[:, ks], k_ref[:, ks], v_ref[:, vs], ig, lf,
                                              c_s[hh], n_s[hh], m_s[hh], scale)
        c_s[hh] = c_new
        n_s[hh] = n_new
        m_s[hh] = m_new
        y_ref[:, vs] = _mlstm_post(h, gn_ref[hh], o_ref[:, vs].astype(F32),
                                   z_ref[:, vs].astype(F32)).astype(y_ref.dtype)

    @pl.when(ci == pl.num_programs(1) - 1)
    def _():
        c_out[...] = c_s[...]
        n_out[...] = n_s[...]
        m_out[...] = m_s[...]


def _mlstm_prompt_call(u, gates, gb, gn, c0, n0, m0, *, batch, seq, heads, dk, dv, cols):
    ln = MLSTM_CHUNK
    nc = seq // ln
    wk, wv = heads * dk, heads * dv
    qb, kb, vb, ob, zb = (cols["q_m"] // wk, cols["k_m"] // wk, cols["v_m"] // wv, cols["o_m"] // wv,
                          cols["z_m"] // wv)
    st4 = lambda b, c: (b, 0, 0, 0)
    return pl.pallas_call(
        functools.partial(_mlstm_prompt_kernel, scale=dk ** -0.5, heads=heads, dk=dk, dv=dv),
        grid=(batch, nc),
        in_specs=[pl.BlockSpec(memory_space=pltpu.SMEM),
                  pl.BlockSpec((ln, wk), lambda b, c: (b * nc + c, qb)),
                  pl.BlockSpec((ln, wk), lambda b, c: (b * nc + c, kb)),
                  pl.BlockSpec((ln, wv), lambda b, c: (b * nc + c, vb)),
                  pl.BlockSpec((ln, wv), lambda b, c: (b * nc + c, ob)),
                  pl.BlockSpec((ln, wv), lambda b, c: (b * nc + c, zb)),
                  pl.BlockSpec((None, None, 2 * heads, ln), lambda b, c: (b, c, 0, 0)),
                  pl.BlockSpec((heads, 1, dv), lambda b, c: (0, 0, 0)),
                  pl.BlockSpec((None, heads, dv, dk), st4),
                  pl.BlockSpec((None, heads, 1, dk), st4),
                  pl.BlockSpec((None, heads, 1, 1), st4)],
        out_specs=[pl.BlockSpec((ln, wv), lambda b, c: (b * nc + c, 0)),
                   pl.BlockSpec((None, heads, dv, dk), st4),
                   pl.BlockSpec((None, heads, 1, dk), st4),
                   pl.BlockSpec((None, heads, 1, 1), st4)],
        out_shape=[jax.ShapeDtypeStruct((batch * seq, wv), BF16),
                   jax.ShapeDtypeStruct((batch, heads, dv, dk), F32),
                   jax.ShapeDtypeStruct((batch, heads, 1, dk), F32),
                   jax.ShapeDtypeStruct((batch, heads, 1, 1), F32)],
        scratch_shapes=[pltpu.VMEM((heads, dv, dk), F32), pltpu.VMEM((heads, 1, dk), F32),
                        pltpu.VMEM((heads, 1, 1), F32)],
        compiler_params=_cparams(("parallel", "arbitrary")),
        name="mlstm_prompt",
    )(gb, u, u, u, u, u, gates, gn, c0, n0, m0)


def _mlstm_sample_kernel(gb_ref, q_ref, k_ref, v_ref, o_ref, z_ref, g_ref, gn_ref, c0_ref, n0_ref, m0_ref,
                         y_ref, c_out, n_out, m_out, *, scale, bb, tq):
    hh = pl.program_id(1)
    q = q_ref[...].astype(F32)
    k = k_ref[...].astype(F32)
    v = v_ref[...].astype(F32)
    o = o_ref[...].astype(F32)
    z = z_ref[...].astype(F32)
    gain = gn_ref[...]
    for i in range(bb):
        rs = slice(i * tq, (i + 1) * tq)
        g = g_ref[i]
        ig = g[0:1, :] + gb_ref[0, hh]
        lf = _log_sigmoid(g[1:2, :] + gb_ref[1, hh])
        h, c_new, n_new, m_new = _mlstm_chunk(q[rs].astype(BF16), k[rs].astype(BF16), v[rs].astype(BF16),
                                              ig, lf, c0_ref[i], n0_ref[i], m0_ref[i], scale)
        y_ref[rs, :] = _mlstm_post(h, gain, o[rs], z[rs]).astype(y_ref.dtype)
        c_out[i] = c_new
        n_out[i] = n_new
        m_out[i] = m_new


def _mlstm_sample_call(u, gates, gb, gn, c0, n0, m0, *, tq, heads, dk, dv, cols):
    bd = c0.shape[0]
    bb = MLSTM_DEC_BB
    qb, kb, vb, ob, zb = (cols["q_m"] // dk, cols["k_m"] // dk, cols["v_m"] // dv, cols["o_m"] // dv,
                          cols["z_m"] // dv)
    st4 = lambda g, h: (g, h, 0, 0)
    return pl.pallas_call(
        functools.partial(_mlstm_sample_kernel, scale=dk ** -0.5, bb=bb, tq=tq),
        grid=(bd // bb, heads),
        in_specs=[pl.BlockSpec(memory_space=pltpu.SMEM),
                  pl.BlockSpec((bb * tq, dk), lambda g, h: (g, qb + h)),
                  pl.BlockSpec((bb * tq, dk), lambda g, h: (g, kb + h)),
                  pl.BlockSpec((bb * tq, dv), lambda g, h: (g, vb + h)),
                  pl.BlockSpec((bb * tq, dv), lambda g, h: (g, ob + h)),
                  pl.BlockSpec((bb * tq, dv), lambda g, h: (g, zb + h)),
                  pl.BlockSpec((bb, None, 2, tq), st4),
                  pl.BlockSpec((None, 1, dv), lambda g, h: (h, 0, 0)),
                  pl.BlockSpec((bb, None, dv, dk), st4),
                  pl.BlockSpec((bb, None, 1, dk), st4),
                  pl.BlockSpec((bb, None, 1, 1), st4)],
        out_specs=[pl.BlockSpec((bb * tq, dv), lambda g, h: (g, h)),
                   pl.BlockSpec((bb, None, dv, dk), st4),
                   pl.BlockSpec((bb, None, 1, dk), st4),
                   pl.BlockSpec((bb, None, 1, 1), st4)],
        out_shape=[jax.ShapeDtypeStruct((bd * tq, heads * dv), F32),
                   jax.ShapeDtypeStruct((bd, heads, dv, dk), F32),
                   jax.ShapeDtypeStruct((bd, heads, 1, dk), F32),
                   jax.ShapeDtypeStruct((bd, heads, 1, 1), F32)],
        compiler_params=_cparams(("parallel", "parallel")),
        name="mlstm_sample",
    )(gb, u, u, u, u, u, gates, gn, c0, n0, m0)


def _out_kernel(x_ref, ya_ref, ym_ref, ga_ref, gm_ref, gate_ref, np_ref, wa_ref, wm_ref, wo_ref, o_ref,
                *, rowwise):
    ya = jnp.dot(ya_ref[...].astype(BF16), wa_ref[...], preferred_element_type=F32)
    ym = jnp.dot(ym_ref[...].astype(BF16), wm_ref[...], preferred_element_type=F32)
    y2 = _sigmoid(ga_ref[...].astype(F32)) * ya + _sigmoid(gm_ref[...].astype(F32)) * ym
    y = jnp.dot(y2.astype(BF16), wo_ref[...], preferred_element_type=F32)
    yn = y * lax.rsqrt(jnp.mean(y * y, axis=-1, keepdims=True) + EPS) * np_ref[...]
    gate = gate_ref[...] if rowwise else gate_ref[0]
    o_ref[...] = x_ref[...] + gate * yn


def _out_call(x, ya, ym, u, gate, norm_post, wa, wm, wo, *, rowwise, rows_per_group, tm, cols):
    t, d = x.shape
    da = ya.shape[1]
    ga_blk, gm_blk = cols["g_a"] // d, cols["g_m"] // d
    if rowwise:
        gspec = pl.BlockSpec((tm, d), lambda i: (i, 0))
    else:
        gspec = pl.BlockSpec((1, 1, d), lambda i: ((i * tm) // rows_per_group, 0, 0))
    resident = dict(pipeline_mode=pl.Buffered(1))
    return pl.pallas_call(
        functools.partial(_out_kernel, rowwise=rowwise),
        grid=(t // tm,),
        in_specs=[pl.BlockSpec((tm, d), lambda i: (i, 0)),
                  pl.BlockSpec((tm, da), lambda i: (i, 0)),
                  pl.BlockSpec((tm, da), lambda i: (i, 0)),
                  pl.BlockSpec((tm, d), lambda i: (i, ga_blk)),
                  pl.BlockSpec((tm, d), lambda i: (i, gm_blk)),
                  gspec,
                  pl.BlockSpec((1, d), lambda i: (0, 0)),
                  pl.BlockSpec(wa.shape, lambda i: (0, 0), **resident),
                  pl.BlockSpec(wm.shape, lambda i: (0, 0), **resident),
                  pl.BlockSpec(wo.shape, lambda i: (0, 0), **resident)],
        out_specs=pl.BlockSpec((tm, d), lambda i: (i, 0)),
        out_shape=jax.ShapeDtypeStruct((t, d), F32),
        compiler_params=_cparams(("parallel",)),
        name="out_stage",
    )(x, ya, ym, u, u, gate, norm_post, wa, wm, wo)


def kernel(x_prompt, x_sample, cache_k, cache_v, state_C, state_n, state_m, page_table, c_prompt, c_sample,
           w_ada, b_ada, norm_pre, norm_post, w_in, b_igate, b_fgate, lambda_q1, lambda_k1, lambda_q2,
           lambda_k2, attn_head_norm, mlstm_head_norm, w_br_a, w_br_m, w_out):
    depth = w_in.shape[0]
    assert depth == 1, "single-layer trunk"
    batch, seq, d = x_prompt.shape
    bd, tq, _ = x_sample.shape
    heads_a, e = cache_k.shape[3], cache_k.shape[4]
    hd = e // 2
    d_att = heads_a * e
    heads_m, dv, dk = state_C.shape[2], state_C.shape[3], state_C.shape[4]
    d_mv, d_mqk = heads_m * dv, heads_m * dk
    layer = 0
    lam_init = 0.8 - 0.6 * math.exp(-0.3 * layer)

    names = ("q_a", "k_a", "v_a", "z_a", "q_m", "k_m", "v_m", "o_m", "z_m", "i_m", "f_m", "g_a", "g_m")
    widths = (d_att, d_att, d_att, d_att, d_mqk, d_mqk, d_mv, d_mv, d_mv, heads_m, heads_m, d, d)
    src = dict(zip(names, np.concatenate([[0], np.cumsum(widths)[:-1]]).tolist()))
    main_cols = src["i_m"]
    cols = {nm: src[nm] for nm in names[:9]}
    cols["g_a"], cols["g_m"] = main_cols, main_cols + d

    w = w_in.reshape(w_in.shape[1:])
    w_tail = w[:, src["g_a"]:].astype(BF16)
    w_g = jnp.concatenate([w[:, src["i_m"]:src["g_a"]], jnp.zeros((d, LANE - 2 * heads_m), F32)],
                          axis=1).astype(BF16)
    inproj = functools.partial(_inproj_call, main_cols=main_cols, k_col=src["k_a"], v_col=src["v_a"],
                               kv_cols=d_att, tn=512)
    wa = w_br_a[layer].astype(BF16)
    wm = w_br_m[layer].astype(BF16)
    wo = w_out[layer].astype(BF16)

    n_c = batch + bd
    pad = (-n_c) % 8
    c_all = jnp.concatenate([c_prompt, c_sample, jnp.zeros((pad, d), F32)], axis=0)
    mod = _mod_call(c_all, w_ada[layer], b_ada[layer].reshape(1, -1))
    shift, scale, gate = mod[:, :d], mod[:, d:2 * d], mod[:, 2 * d:]

    lamv = jnp.stack([lambda_q1[layer], lambda_k1[layer], lambda_q2[layer], lambda_k2[layer]])
    gn_a = attn_head_norm[layer].reshape(heads_a, e)
    gn_m = mlstm_head_norm[layer].reshape(heads_m, 1, dv)
    gb = jnp.stack([b_igate[layer], b_fgate[layer]])
    npre = norm_pre[layer].reshape(1, d)
    npost = norm_post[layer].reshape(1, d)

    xp = x_prompt.reshape(batch * seq, d)
    grp = lambda a: a[:batch].reshape(batch, 1, d)
    u_p, k_p, v_p, g_p, vt_p = inproj(xp, grp(scale), grp(shift), npre, w, w_tail, w_g, rowwise=False,
                                      rows_per_group=seq, tm=1024, u_dtype=BF16, want_vt=True, x_buffers=1)
    ya_p = _attn_prompt_call(u_p, vt_p, lamv, gn_a.reshape(heads_a, e, 1), batch=batch, seq=seq,
                             heads=heads_a, hd=hd, lam_init=lam_init, cols=cols)
    nc = seq // MLSTM_CHUNK
    gates_p = g_p[:, :2 * heads_m].reshape(batch, nc, MLSTM_CHUNK, 2 * heads_m).transpose(0, 1, 3, 2)
    ym_p, c_p, n_p, m_p = _mlstm_prompt_call(
        u_p, gates_p, gb, gn_m, jnp.zeros((batch, heads_m, dv, dk), F32), jnp.zeros((batch, heads_m, 1, dk), F32),
        jnp.zeros((batch, heads_m, 1, 1), F32), batch=batch, seq=seq, heads=heads_m, dk=dk, dv=dv, cols=cols)
    y_p = _out_call(xp, ya_p, ym_p, u_p, grp(gate), npost, wa, wm, wo, rowwise=False, rows_per_group=seq,
                    tm=512, cols=cols)

    xs = x_sample.reshape(bd * tq, d)
    rep = lambda a: jnp.repeat(a[batch:batch + bd], tq, axis=0)
    u_s, k_s, v_s, g_s = inproj(xs, rep(scale), rep(shift), npre, w, w_tail, w_g, rowwise=True,
                                rows_per_group=tq, tm=min(512, bd * tq), u_dtype=F32, want_vt=False,
                                x_buffers=2)
    ya_s = _attn_sample_call(u_s, k_s, v_s, cache_k.reshape(cache_k.shape[1:]),
                             cache_v.reshape(cache_v.shape[1:]), page_table, lamv, gn_a,
                             tq=tq, heads=heads_a, hd=hd, lam_init=lam_init, cols=cols)
    gates_s = g_s[:, :2 * heads_m].reshape(bd, tq, 2, heads_m).transpose(0, 3, 2, 1)
    ym_s, c_s, n_s, m_s = _mlstm_sample_call(
        u_s, gates_s, gb, gn_m, state_C.reshape(state_C.shape[1:]), state_n.reshape(bd, heads_m, 1, dk),
        state_m.reshape(bd, heads_m, 1, 1), tq=tq, heads=heads_m, dk=dk, dv=dv, cols=cols)
    y_s = _out_call(xs, ya_s, ym_s, u_s, rep(gate), npost, wa, wm, wo, rowwise=True, rows_per_group=tq,
                    tm=min(256, bd * tq), cols=cols)

    return (y_p.reshape(batch, seq, d), y_s.reshape(bd, tq, d),
            k_p.reshape(1, batch, seq, heads_a, e), v_p.reshape(1, batch, seq, heads_a, e),
            c_p.reshape(1, batch, heads_m, dv, dk), n_p.reshape(1, batch, heads_m, dk),
            m_p.reshape(1, batch, heads_m),
            k_s.reshape(1, bd, tq, heads_a, e), v_s.reshape(1, bd, tq, heads_a, e),
            c_s.reshape(1, bd, heads_m, dv, dk), n_s.reshape(1, bd, heads_m, dk), m_s.reshape(1, bd, heads_m))
```

```python
import functools
import math

import jax
import jax.numpy as jnp
import numpy as np
from jax import lax
from jax.experimental import pallas as pl
from jax.experimental.pallas import tpu as pltpu

F32 = jnp.float32
BF16 = jnp.bfloat16
EPS = 1e-6
NEG = -1e30
VMEM_LIMIT = 56 * 1024 * 1024
LANE = 128
MLSTM_CHUNK = 256
ATT_BLOCK = 512
PAGES_PER_STEP = 16
MLSTM_DEC_BB = 16

_NT = (((1,), (1,)), ((), ()))
_TN = (((0,), (0,)), ((), ()))


def _cparams(sem):
    return pltpu.CompilerParams(dimension_semantics=sem, vmem_limit_bytes=VMEM_LIMIT)


def _sigmoid(x):
    return 1.0 / (1.0 + jnp.exp(-x))


def _silu(x):
    return x * _sigmoid(x)


def _mod_kernel(c_ref, w_ref, b_ref, o_ref):
    a = _silu(c_ref[...]).astype(BF16)
    o_ref[...] = jnp.dot(a, w_ref[...].astype(BF16), preferred_element_type=F32) + b_ref[...]


def _mod_call(c_all, w_ada, b_ada):
    rows, d = c_all.shape
    n = w_ada.shape[1]
    tn = 512
    return pl.pallas_call(
        _mod_kernel,
        grid=(n // tn,),
        in_specs=[pl.BlockSpec((rows, d), lambda j: (0, 0)),
                  pl.BlockSpec((d, tn), lambda j: (0, j)),
                  pl.BlockSpec((1, tn), lambda j: (0, j))],
        out_specs=pl.BlockSpec((rows, tn), lambda j: (0, j)),
        out_shape=jax.ShapeDtypeStruct((rows, n), F32),
        compiler_params=_cparams(("parallel",)),
        name="mod",
    )(c_all, w_ada, b_ada)


def _inproj_kernel(x_ref, sc_ref, sh_ref, g_ref, w_ref, wg_ref, u_ref, k_ref, v_ref, gate_ref, *rest,
                   rowwise, kj0, vj0, nkv, want_vt):
    if want_vt:
        vt_ref, h_ref = rest
    else:
        (h_ref,) = rest
    j = pl.program_id(1)

    @pl.when(j == 0)
    def _():
        x = x_ref[...]
        xn = x * lax.rsqrt(jnp.mean(x * x, axis=-1, keepdims=True) + EPS) * g_ref[...]
        sc = sc_ref[...] if rowwise else sc_ref[0]
        sh = sh_ref[...] if rowwise else sh_ref[0]
        h = (xn * (1.0 + sc) + sh).astype(BF16)
        h_ref[...] = h
        gate_ref[...] = lax.dot_general(h, wg_ref[...].astype(BF16), _NT, preferred_element_type=F32)

    res = lax.dot_general(h_ref[...], w_ref[...].astype(BF16), _NT, preferred_element_type=F32)
    u_ref[...] = res.astype(u_ref.dtype)

    @pl.when((j >= kj0) & (j < kj0 + nkv))
    def _():
        k_ref[...] = res

    @pl.when((j >= vj0) & (j < vj0 + nkv))
    def _():
        v_ref[...] = res
        if want_vt:
            vt_ref[...] = res.T.astype(BF16)


def _inproj_call(x, scale, shift, norm_pre, wt, *, rowwise, rows_per_group, tm, tn, u_dtype,
                 main_cols, gate_cols, k_col, v_col, kv_cols, want_vt, x_buffers):
    t, d = x.shape
    n_main = main_cols // tn
    n_tiles = (wt.shape[0] - gate_cols) // tn
    kj0, vj0, nkv = k_col // tn, v_col // tn, kv_cols // tn
    clamp = lambda j, j0: jnp.clip(j - j0, 0, nkv - 1)
    if rowwise:
        mspec = pl.BlockSpec((tm, d), lambda i, j: (i, 0))
    else:
        mspec = pl.BlockSpec((1, 1, d), lambda i, j: ((i * tm) // rows_per_group, 0, 0))
    out_specs = [pl.BlockSpec((tm, tn), lambda i, j: (i, j)),
                 pl.BlockSpec((tm, tn), lambda i, j: (i, clamp(j, kj0))),
                 pl.BlockSpec((tm, tn), lambda i, j: (i, clamp(j, vj0))),
                 pl.BlockSpec((tm, gate_cols), lambda i, j: (i, 0))]
    out_shape = [jax.ShapeDtypeStruct((t, n_tiles * tn), u_dtype),
                 jax.ShapeDtypeStruct((t, kv_cols), F32), jax.ShapeDtypeStruct((t, kv_cols), F32),
                 jax.ShapeDtypeStruct((t, gate_cols), F32)]
    if want_vt:
        out_specs.append(pl.BlockSpec((tn, tm), lambda i, j: (clamp(j, vj0), i)))
        out_shape.append(jax.ShapeDtypeStruct((kv_cols, t), BF16))
    return pl.pallas_call(
        functools.partial(_inproj_kernel, rowwise=rowwise, kj0=kj0, vj0=vj0, nkv=nkv, want_vt=want_vt),
        grid=(t // tm, n_tiles),
        in_specs=[pl.BlockSpec((tm, d), lambda i, j: (i, 0), pipeline_mode=pl.Buffered(x_buffers)),
                  mspec, mspec,
                  pl.BlockSpec((1, d), lambda i, j: (0, 0)),
                  pl.BlockSpec((pl.Element(tn), pl.Element(d)),
                               lambda i, j: (pl.multiple_of(j * tn + jnp.where(j >= n_main, gate_cols, 0), 8), 0)),
                  pl.BlockSpec((gate_cols, d), lambda i, j: (main_cols // gate_cols, 0))],
        out_specs=out_specs,
        out_shape=out_shape,
        scratch_shapes=[pltpu.VMEM((tm, d), BF16)],
        compiler_params=_cparams(("parallel", "arbitrary")),
        name="inproj",
    )(x, scale, shift, norm_pre, wt, wt)


def _lambda_value(lamv_ref, lam_init):
    lv = lamv_ref[...]
    d1 = jnp.sum(lv[0:1] * lv[1:2], axis=-1, keepdims=True)
    d2 = jnp.sum(lv[2:3] * lv[3:4], axis=-1, keepdims=True)
    return jnp.exp(d1) - jnp.exp(d2) + lam_init


def _softmax_step(s, shift, v, m_ref, l_ref, acc_ref):
    m_old = m_ref[...]
    m_new = jnp.maximum(m_old, jnp.max(s, axis=-1, keepdims=True) + shift)
    alpha = jnp.exp(m_old - m_new)
    p = jnp.exp(s - (m_new - shift))
    l_ref[...] = alpha * l_ref[...] + jnp.sum(p, axis=-1, keepdims=True)
    acc_ref[...] = alpha * acc_ref[...] + jnp.dot(p.astype(BF16), v, preferred_element_type=F32)
    m_ref[...] = m_new


def _head_post(att, gain, lam_init, z):
    r = att * lax.rsqrt(jnp.mean(att * att, axis=-1, keepdims=True) + EPS) * gain
    return r * (1.0 - lam_init) * _silu(z)


def _attn_prompt_kernel(qi_ref, kj_ref, q_ref, k_ref, vt_ref, z_ref, mask_ref, slope_ref, lamv_ref,
                        gn_ref, o_ref, qs_ref, ka_ref, m_ref, l_ref, acc_ref, *, blk, hd, lam_init):
    e = 2 * hd
    split = 64
    h = pl.program_id(1)
    p = pl.program_id(2)
    i = qi_ref[p]
    j = kj_ref[p]
    slope = slope_ref[h]

    @pl.when(j == 0)
    def _():
        q = q_ref[...].astype(F32) * (hd ** -0.5)
        lane = lax.broadcasted_iota(jnp.int32, q.shape, 1)
        idx = lax.broadcasted_iota(jnp.int32, q.shape, 0)
        hi = (idx // split).astype(F32)
        lo = (idx % split).astype(F32)
        qs_ref[0, :, 0:e] = jnp.where(lane < hd, q, 0.0).astype(BF16)
        qs_ref[1, :, 0:e] = jnp.where(lane >= hd, q, 0.0).astype(BF16)
        qf = jnp.where(lane == 0, split * slope,
                       jnp.where(lane == 1, slope,
                                 jnp.where(lane == 2, -split * slope * hi,
                                           jnp.where(lane == 3, -slope * lo, 0.0)))).astype(BF16)
        qs_ref[0, :, e:2 * e] = qf
        qs_ref[1, :, e:2 * e] = qf
        ka_ref[:, e:2 * e] = jnp.where(lane == 0, hi, jnp.where(lane == 1, lo,
                                                                 jnp.where(lane < 4, 1.0, 0.0))).astype(BF16)
        m_ref[...] = jnp.full(m_ref.shape, NEG, F32)
        l_ref[...] = jnp.zeros(l_ref.shape, F32)
        acc_ref[...] = jnp.zeros(acc_ref.shape, F32)

    shift = slope * ((j - i) * blk).astype(F32)
    ka_ref[:, 0:e] = k_ref[...]

    def step(causal):
        ka = ka_ref[...]
        vt = vt_ref[...]
        for c in range(2):
            s = lax.dot_general(ka, qs_ref[c], _NT, preferred_element_type=F32)
            if causal:
                s = s + mask_ref[...]
            m_old = m_ref[c]
            m_new = jnp.maximum(m_old, jnp.max(s, axis=0, keepdims=True) + shift)
            alpha = jnp.exp(m_old - m_new)
            pr = jnp.exp(s - (m_new - shift))
            l_ref[c] = alpha * l_ref[c] + jnp.sum(pr, axis=0, keepdims=True)
            acc_ref[c] = alpha * acc_ref[c] + jnp.dot(vt, pr.astype(BF16), preferred_element_type=F32)
            m_ref[c] = m_new

    @pl.when(j != i)
    def _():
        step(False)

    @pl.when(j == i)
    def _():
        step(True)
        lam = _lambda_value(lamv_ref, lam_init)
        att = acc_ref[0] / l_ref[0] - lam * (acc_ref[1] / l_ref[1])
        r = att * lax.rsqrt(jnp.mean(att * att, axis=0, keepdims=True) + EPS) * gn_ref[...]
        o_ref[...] = (r.T * (1.0 - lam_init) * _silu(z_ref[...].astype(F32))).astype(o_ref.dtype)


def _attn_prompt_call(u, vt, lamv, gn, *, batch, seq, heads, hd, lam_init, cols):
    blk = ATT_BLOCK
    nb = seq // blk
    qi = np.array([i for i in range(nb) for _ in range(i + 1)], np.int32)
    kj = np.array([j for i in range(nb) for j in range(i + 1)], np.int32)
    assert 8 % heads == 0, "ALiBi slopes must be powers of two for the exact in-contraction bias"
    slopes = 2.0 ** (-8.0 * np.arange(1, heads + 1, dtype=np.float32) / heads)
    rel = np.arange(blk)[:, None] - np.arange(blk)[None, :]
    mask = jnp.asarray(np.where(rel <= 0, 0.0, NEG).astype(np.float32))
    e = 2 * hd
    qb, kb, zb = cols["q_a"] // e, cols["k_a"] // e, cols["z_a"] // e

    grid_spec = pltpu.PrefetchScalarGridSpec(
        num_scalar_prefetch=2,
        grid=(batch, heads, len(qi)),
        in_specs=[
            pl.BlockSpec((blk, e), lambda b, h, p, qi, kj: (b * nb + qi[p], qb + h)),
            pl.BlockSpec((blk, e), lambda b, h, p, qi, kj: (b * nb + kj[p], kb + h)),
            pl.BlockSpec((e, blk), lambda b, h, p, qi, kj: (h, b * nb + kj[p])),
            pl.BlockSpec((blk, e), lambda b, h, p, qi, kj: (b * nb + qi[p], zb + h)),
            pl.BlockSpec((blk, blk), lambda b, h, p, qi, kj: (0, 0)),
            pl.BlockSpec(memory_space=pltpu.SMEM),
            pl.BlockSpec(lamv.shape, lambda b, h, p, qi, kj: (0, 0)),
            pl.BlockSpec((None, e, 1), lambda b, h, p, qi, kj: (h, 0, 0)),
        ],
        out_specs=pl.BlockSpec((blk, e), lambda b, h, p, qi, kj: (b * nb + qi[p], h)),
        scratch_shapes=[pltpu.VMEM((2, blk, 2 * e), BF16), pltpu.VMEM((blk, 2 * e), BF16),
                        pltpu.VMEM((2, 1, blk), F32), pltpu.VMEM((2, 1, blk), F32),
                        pltpu.VMEM((2, e, blk), F32)],
    )
    return pl.pallas_call(
        functools.partial(_attn_prompt_kernel, blk=blk, hd=hd, lam_init=lam_init),
        grid_spec=grid_spec,
        out_shape=jax.ShapeDtypeStruct((batch * seq, heads * e), BF16),
        compiler_params=_cparams(("parallel", "parallel", "arbitrary")),
        name="attn_prompt",
    )(jnp.asarray(qi), jnp.asarray(kj), u, u, vt, u, mask, jnp.asarray(slopes), lamv, gn)


def _attn_sample_kernel(pt_ref, q_ref, z_ref, kn_ref, vn_ref, *rest, pps, heads, hd, tq, page, lam_init):
    kp_refs = rest[:pps]
    vp_refs = rest[pps:2 * pps]
    (bp_ref, bn_ref, slope_ref, lamv_ref, gn_ref, o_ref, qb_ref, s_ref, m_ref, l_ref, acc_ref) = rest[2 * pps:]
    j = pl.program_id(1)
    e = 2 * hd
    nr = heads * tq
    pw = page * heads

    @pl.when(j == 0)
    def _():
        q = q_ref[...].astype(F32) * (hd ** -0.5)
        lane = lax.broadcasted_iota(jnp.int32, (tq, e), 1)
        for hh in range(heads):
            qh = q[:, hh * e:(hh + 1) * e]
            qb_ref[hh * tq:(hh + 1) * tq, :] = jnp.where(lane < hd, qh, 0.0)
            qb_ref[nr + hh * tq:nr + (hh + 1) * tq, :] = jnp.where(lane >= hd, qh, 0.0)
        m_ref[...] = jnp.full(m_ref.shape, NEG, F32)
        l_ref[...] = jnp.zeros(l_ref.shape, F32)
        acc_ref[...] = jnp.zeros(acc_ref.shape, F32)

    qb = qb_ref[...].astype(BF16)
    shifts = [slope_ref[...] * ((j * pps + s) * page).astype(F32) for s in range(pps)]
    mx = None
    for s in range(pps):
        kp = kp_refs[s][...].reshape(pw, e).astype(BF16)
        sc = lax.dot_general(qb, kp, _NT, preferred_element_type=F32) + bp_ref[...]
        s_ref[:, s * pw:(s + 1) * pw] = sc
        ms = jnp.max(sc, axis=-1, keepdims=True) + shifts[s]
        mx = ms if mx is None else jnp.maximum(mx, ms)
    m_old = m_ref[...]
    m_new = jnp.maximum(m_old, mx)
    alpha = jnp.exp(m_old - m_new)
    lsum = jnp.zeros_like(m_old)
    pv = jnp.zeros(acc_ref.shape, F32)
    for s in range(pps):
        pr = jnp.exp(s_ref[:, s * pw:(s + 1) * pw] - (m_new - shifts[s]))
        lsum = lsum + jnp.sum(pr, axis=-1, keepdims=True)
        vp = vp_refs[s][...].reshape(pw, e).astype(BF16)
        pv = pv + jnp.dot(pr.astype(BF16), vp, preferred_element_type=F32)
    l_ref[...] = alpha * l_ref[...] + lsum
    acc_ref[...] = alpha * acc_ref[...] + pv
    m_ref[...] = m_new

    @pl.when(j == pl.num_programs(1) - 1)
    def _():
        kn = kn_ref[...]
        vn = vn_ref[...]
        knr = jnp.concatenate([kn[:, hh * e:(hh + 1) * e] for hh in range(heads)], axis=0).astype(BF16)
        vnr = jnp.concatenate([vn[:, hh * e:(hh + 1) * e] for hh in range(heads)], axis=0).astype(BF16)
        sc = lax.dot_general(qb, knr, _NT, preferred_element_type=F32) + bn_ref[...]
        _softmax_step(sc, 0.0, vnr, m_ref, l_ref, acc_ref)
        o = acc_ref[...] / l_ref[...]
        lam = _lambda_value(lamv_ref, lam_init)
        att = o[0:nr] - lam * o[nr:2 * nr]
        z = z_ref[...].astype(F32)
        for hh in range(heads):
            r = _head_post(att[hh * tq:(hh + 1) * tq], gn_ref[hh:hh + 1, :], lam_init,
                           z[:, hh * e:(hh + 1) * e])
            o_ref[:, hh * e:(hh + 1) * e] = r.astype(o_ref.dtype)


def _attn_sample_call(u, k_new, v_new, cache_k, cache_v, page_table, lamv, gn, *, tq, heads, hd,
                      lam_init, cols):
    bd, npg = page_table.shape
    page = cache_k.shape[1]
    pps = PAGES_PER_STEP
    e = 2 * hd
    nr = heads * tq
    past = npg * page
    slopes = 2.0 ** (-8.0 * np.arange(1, heads + 1, dtype=np.float32) / heads)
    r_head = (np.arange(2 * nr) % nr) // tq
    r_tok = np.arange(2 * nr) % tq
    r_slope = slopes[r_head]
    c_tok, c_head = np.arange(page * heads) // heads, np.arange(page * heads) % heads
    bp = r_slope[:, None] * (c_tok[None, :] - (past + r_tok)[:, None])
    bp = np.where(r_head[:, None] == c_head[None, :], bp, NEG).astype(np.float32)
    n_head, n_tok = np.arange(nr) // tq, np.arange(nr) % tq
    bn = r_slope[:, None] * (n_tok[None, :] - r_tok[:, None])
    ok = (r_head[:, None] == n_head[None, :]) & (n_tok[None, :] <= r_tok[:, None])
    bn = np.where(ok, bn, NEG).astype(np.float32)
    slope_col = r_slope.reshape(2 * nr, 1).astype(np.float32)
    w = heads * e

    def page_spec(s):
        return pl.BlockSpec((None, page, heads, e),
                            lambda b, j, pt: (pt[b * npg + j * pps + s], 0, 0, 0))

    const2 = lambda b, j, pt: (0, 0)
    grid_spec = pltpu.PrefetchScalarGridSpec(
        num_scalar_prefetch=1,
        grid=(bd, npg // pps),
        in_specs=[pl.BlockSpec((tq, w), lambda b, j, pt: (b, cols["q_a"] // w)),
                  pl.BlockSpec((tq, w), lambda b, j, pt: (b, cols["z_a"] // w)),
                  pl.BlockSpec((tq, w), lambda b, j, pt: (b, 0)),
                  pl.BlockSpec((tq, w), lambda b, j, pt: (b, 0))]
                 + [page_spec(s) for s in range(pps)] + [page_spec(s) for s in range(pps)]
                 + [pl.BlockSpec(bp.shape, const2), pl.BlockSpec(bn.shape, const2),
                    pl.BlockSpec(slope_col.shape, const2), pl.BlockSpec(lamv.shape, const2),
                    pl.BlockSpec(gn.shape, const2)],
        out_specs=pl.BlockSpec((tq, w), lambda b, j, pt: (b, 0)),
        scratch_shapes=[pltpu.VMEM((2 * nr, e), F32), pltpu.VMEM((2 * nr, pps * page * heads), F32),
                        pltpu.VMEM((2 * nr, 1), F32), pltpu.VMEM((2 * nr, 1), F32),
                        pltpu.VMEM((2 * nr, e), F32)],
    )
    return pl.pallas_call(
        functools.partial(_attn_sample_kernel, pps=pps, heads=heads, hd=hd, tq=tq, page=page,
                          lam_init=lam_init),
        grid_spec=grid_spec,
        out_shape=jax.ShapeDtypeStruct((bd * tq, w), F32),
        compiler_params=_cparams(("parallel", "arbitrary")),
        name="attn_sample",
    )(page_table.reshape(-1), u, u, k_new, v_new, *([cache_k] * pps), *([cache_v] * pps),
      jnp.asarray(bp), jnp.asarray(bn), jnp.asarray(slope_col), lamv, gn)


def _mlstm_chunks(chains, scale):
    ln = chains[0][0].shape[0]
    row = lax.broadcasted_iota(jnp.int32, (ln, ln), 0)
    col = lax.broadcasted_iota(jnp.int32, (ln, ln), 1)
    tri = col <= row
    eye = col == row

    gate = []
    for (q, k, v, ig_row, lf_row, c_st, n_st, m_st) in chains:
        lf_b = jnp.broadcast_to(lf_row, (ln, ln))
        ig_b = jnp.broadcast_to(ig_row, (ln, ln))
        b_col = jnp.sum(jnp.where(tri, lf_b, 0.0), axis=1, keepdims=True)
        lf_col = jnp.sum(jnp.where(eye, lf_b, 0.0), axis=1, keepdims=True)
        ig_col = jnp.sum(jnp.where(eye, ig_b, 0.0), axis=1, keepdims=True)
        b_row = jnp.sum(jnp.where(row <= col, jnp.broadcast_to(lf_col, (ln, ln)), 0.0),
                        axis=0, keepdims=True)
        dmat = jnp.where(tri, b_col - b_row + ig_row, NEG)
        a_col = b_col + m_st
        m_t = jnp.maximum(a_col, jnp.max(dmat, axis=1, keepdims=True))
        m_new = m_t[ln - 1:ln, :]
        b_last = b_col[ln - 1:ln, :]
        wk = jnp.exp(b_last - b_col + ig_col - m_new)
        decay = jnp.exp(b_last + m_st - m_new)
        gate.append((jnp.exp(dmat - m_t), jnp.exp(a_col - m_t), m_t, m_new, wk, decay))

    first = []
    for (q, k, v, _, _, c_st, _, _), (_, _, _, _, wk, _) in zip(chains, gate):
        qk = lax.dot_general(q, k, _NT, preferred_element_type=F32)
        q_c = lax.dot_general(q, c_st.astype(BF16), _NT, preferred_element_type=F32)
        vw = (v.astype(F32) * wk).astype(BF16)
        vk = lax.dot_general(vw, k, _TN, preferred_element_type=F32)
        first.append((qk, q_c, vk))

    out = []
    for (q, k, v, _, _, c_st, n_st, _), (dexp, inter, m_t, m_new, wk, decay), (qk, q_c, vk) in zip(
            chains, gate, first):
        s = qk * scale * dexp
        num = inter * q_c + jnp.dot(s.astype(BF16), v, preferred_element_type=F32)
        den = (inter * jnp.sum(q.astype(F32) * n_st, axis=1, keepdims=True)
               + jnp.sum(s, axis=1, keepdims=True))
        h = num / jnp.maximum(jnp.abs(den), jnp.exp(-m_t))
        c_new = decay * c_st + vk * scale
        n_new = decay * n_st + jnp.sum(k.astype(F32) * wk, axis=0, keepdims=True) * scale
        out.append((h, c_new, n_new, m_new))
    return out


def _log_sigmoid(x):
    return jnp.minimum(x, 0.0) - jnp.log(1.0 + jnp.exp(-jnp.abs(x)))


def _mlstm_post(h, gain, o, z):
    hn = h * lax.rsqrt(jnp.mean(h * h, axis=-1, keepdims=True) + EPS) * gain
    return _sigmoid(o) * hn * _silu(z)


def _mlstm_prompt_kernel(gb_ref, q_ref, k_ref, v_ref, o_ref, z_ref, g_ref, gn_ref, c0_ref, n0_ref, m0_ref,
                         y_ref, c_out, n_out, m_out, c_s, n_s, m_s, *, scale, heads, dk, dv):
    ci = pl.program_id(1)

    @pl.when(ci == 0)
    def _():
        c_s[...] = c0_ref[...]
        n_s[...] = n0_ref[...]
        m_s[...] = m0_ref[...]

    g = g_ref[...]
    chains = []
    for hh in range(heads):
        ks = slice(hh * dk, (hh + 1) * dk)
        vs = slice(hh * dv, (hh + 1) * dv)
        ig = g[hh:hh + 1, :] + gb_ref[0, hh]
        lf = _log_sigmoid(g[heads + hh:heads + hh + 1, :] + gb_ref[1, hh])
        chains.append((q_ref[:, ks], k_ref[:, ks], v_ref[:, vs], ig, lf, c_s[hh], n_s[hh], m_s[hh]))
    for hh, (h, c_new, n_new, m_new) in enumerate(_mlstm_chunks(chains, scale)):
        vs = slice(hh * dv, (hh + 1) * dv)
        c_s[hh] = c_new
        n_s[hh] = n_new
        m_s[hh] = m_new
        y_ref[:, vs] = _mlstm_post(h, gn_ref[hh], o_ref[:, vs].astype(F32),
                                   z_ref[:, vs].astype(F32)).astype(y_ref.dtype)

    @pl.when(ci == pl.num_programs(1) - 1)
    def _():
        c_out[...] = c_s[...]
        n_out[...] = n_s[...]
        m_out[...] = m_s[...]


def _mlstm_prompt_call(u, gates, gb, gn, c0, n0, m0, *, batch, seq, heads, dk, dv, cols):
    ln = MLSTM_CHUNK
    nc = seq // ln
    wk, wv = heads * dk, heads * dv
    qb, kb, vb, ob, zb = (cols["q_m"] // wk, cols["k_m"] // wk, cols["v_m"] // wv, cols["o_m"] // wv,
                          cols["z_m"] // wv)
    st4 = lambda b, c: (b, 0, 0, 0)
    return pl.pallas_call(
        functools.partial(_mlstm_prompt_kernel, scale=dk ** -0.5, heads=heads, dk=dk, dv=dv),
        grid=(batch, nc),
        in_specs=[pl.BlockSpec(memory_space=pltpu.SMEM),
                  pl.BlockSpec((ln, wk), lambda b, c: (b * nc + c, qb)),
                  pl.BlockSpec((ln, wk), lambda b, c: (b * nc + c, kb)),
                  pl.BlockSpec((ln, wv), lambda b, c: (b * nc + c, vb)),
                  pl.BlockSpec((ln, wv), lambda b, c: (b * nc + c, ob)),
                  pl.BlockSpec((ln, wv), lambda b, c: (b * nc + c, zb)),
                  pl.BlockSpec((None, None, 2 * heads, ln), lambda b, c: (b, c, 0, 0)),
                  pl.BlockSpec((heads, 1, dv), lambda b, c: (0, 0, 0)),
                  pl.BlockSpec((None, heads, dv, dk), st4),
                  pl.BlockSpec((None, heads, 1, dk), st4),
                  pl.BlockSpec((None, heads, 1, 1), st4)],
        out_specs=[pl.BlockSpec((ln, wv), lambda b, c: (b * nc + c, 0)),
                   pl.BlockSpec((None, heads, dv, dk), st4),
                   pl.BlockSpec((None, heads, 1, dk), st4),
                   pl.BlockSpec((None, heads, 1, 1), st4)],
        out_shape=[jax.ShapeDtypeStruct((batch * seq, wv), BF16),
                   jax.ShapeDtypeStruct((batch, heads, dv, dk), F32),
                   jax.ShapeDtypeStruct((batch, heads, 1, dk), F32),
                   jax.ShapeDtypeStruct((batch, heads, 1, 1), F32)],
        scratch_shapes=[pltpu.VMEM((heads, dv, dk), F32), pltpu.VMEM((heads, 1, dk), F32),
                        pltpu.VMEM((heads, 1, 1), F32)],
        compiler_params=_cparams(("parallel", "arbitrary")),
        name="mlstm_prompt",
    )(gb, u, u, u, u, u, gates, gn, c0, n0, m0)


def _mlstm_sample_kernel(gb_ref, q_ref, k_ref, v_ref, o_ref, z_ref, g_ref, gn_ref, c0_ref, n0_ref, m0_ref,
                         y_ref, c_out, n_out, m_out, *, scale, bb, tq):
    hh = pl.program_id(1)
    q = q_ref[...].astype(F32)
    k = k_ref[...].astype(F32)
    v = v_ref[...].astype(F32)
    o = o_ref[...].astype(F32)
    z = z_ref[...].astype(F32)
    gain = gn_ref[...]
    chains = []
    for i in range(bb):
        rs = slice(i * tq, (i + 1) * tq)
        g = g_ref[i]
        ig = g[0:1, :] + gb_ref[0, hh]
        lf = _log_sigmoid(g[1:2, :] + gb_ref[1, hh])
        chains.append((q[rs].astype(BF16), k[rs].astype(BF16), v[rs].astype(BF16), ig, lf,
                       c0_ref[i], n0_ref[i], m0_ref[i]))
    for i, (h, c_new, n_new, m_new) in enumerate(_mlstm_chunks(chains, scale)):
        rs = slice(i * tq, (i + 1) * tq)
        y_ref[rs, :] = _mlstm_post(h, gain, o[rs], z[rs]).astype(y_ref.dtype)
        c_out[i] = c_new
        n_out[i] = n_new
        m_out[i] = m_new


def _mlstm_sample_call(u, gates, gb, gn, c0, n0, m0, *, tq, heads, dk, dv, cols):
    bd = c0.shape[0]
    bb = MLSTM_DEC_BB
    assert bd % bb == 0, (bd, bb)
    qb, kb, vb, ob, zb = (cols["q_m"] // dk, cols["k_m"] // dk, cols["v_m"] // dv, cols["o_m"] // dv,
                          cols["z_m"] // dv)
    st4 = lambda g, h: (g, h, 0, 0)
    return pl.pallas_call(
        functools.partial(_mlstm_sample_kernel, scale=dk ** -0.5, bb=bb, tq=tq),
        grid=(bd // bb, heads),
        in_specs=[pl.BlockSpec(memory_space=pltpu.SMEM),
                  pl.BlockSpec((bb * tq, dk), lambda g, h: (g, qb + h)),
                  pl.BlockSpec((bb * tq, dk), lambda g, h: (g, kb + h)),
                  pl.BlockSpec((bb * tq, dv), lambda g, h: (g, vb + h)),
                  pl.BlockSpec((bb * tq, dv), lambda g, h: (g, ob + h)),
                  pl.BlockSpec((bb * tq, dv), lambda g, h: (g, zb + h)),
                  pl.BlockSpec((bb, None, 2, tq), st4),
                  pl.BlockSpec((None, 1, dv), lambda g, h: (h, 0, 0)),
                  pl.BlockSpec((bb, None, dv, dk), st4),
                  pl.BlockSpec((bb, None, 1, dk), st4),
                  pl.BlockSpec((bb, None, 1, 1), st4)],
        out_specs=[pl.BlockSpec((bb * tq, dv), lambda g, h: (g, h)),
                   pl.BlockSpec((bb, None, dv, dk), st4),
                   pl.BlockSpec((bb, None, 1, dk), st4),
                   pl.BlockSpec((bb, None, 1, 1), st4)],
        out_shape=[jax.ShapeDtypeStruct((bd * tq, heads * dv), F32),
                   jax.ShapeDtypeStruct((bd, heads, dv, dk), F32),
                   jax.ShapeDtypeStruct((bd, heads, 1, dk), F32),
                   jax.ShapeDtypeStruct((bd, heads, 1, 1), F32)],
        compiler_params=_cparams(("parallel", "parallel")),
        name="mlstm_sample",
    )(gb, u, u, u, u, u, gates, gn, c0, n0, m0)


def _out_kernel(x_ref, ya_ref, ym_ref, ga_ref, gm_ref, gate_ref, np_ref, wa_ref, wm_ref, wo_ref, o_ref,
                *, rowwise):
    ya = jnp.dot(ya_ref[...].astype(BF16), wa_ref[...], preferred_element_type=F32)
    ym = jnp.dot(ym_ref[...].astype(BF16), wm_ref[...], preferred_element_type=F32)
    y2 = _sigmoid(ga_ref[...].astype(F32)) * ya + _sigmoid(gm_ref[...].astype(F32)) * ym
    y = jnp.dot(y2.astype(BF16), wo_ref[...], preferred_element_type=F32)
    yn = y * lax.rsqrt(jnp.mean(y * y, axis=-1, keepdims=True) + EPS) * np_ref[...]
    gate = gate_ref[...] if rowwise else gate_ref[0]
    o_ref[...] = x_ref[...] + gate * yn


def _out_call(x, ya, ym, u, gate, norm_post, wa, wm, wo, *, rowwise, rows_per_group, tm, cols):
    t, d = x.shape
    da = ya.shape[1]
    ga_blk, gm_blk = cols["g_a"] // d, cols["g_m"] // d
    if rowwise:
        gspec = pl.BlockSpec((tm, d), lambda i: (i, 0))
    else:
        gspec = pl.BlockSpec((1, 1, d), lambda i: ((i * tm) // rows_per_group, 0, 0))
    resident = dict(pipeline_mode=pl.Buffered(1))
    return pl.pallas_call(
        functools.partial(_out_kernel, rowwise=rowwise),
        grid=(t // tm,),
        in_specs=[pl.BlockSpec((tm, d), lambda i: (i, 0)),
                  pl.BlockSpec((tm, da), lambda i: (i, 0)),
                  pl.BlockSpec((tm, da), lambda i: (i, 0)),
                  pl.BlockSpec((tm, d), lambda i: (i, ga_blk)),
                  pl.BlockSpec((tm, d), lambda i: (i, gm_blk)),
                  gspec,
                  pl.BlockSpec((1, d), lambda i: (0, 0)),
                  pl.BlockSpec(wa.shape, lambda i: (0, 0), **resident),
                  pl.BlockSpec(wm.shape, lambda i: (0, 0), **resident),
                  pl.BlockSpec(wo.shape, lambda i: (0, 0), **resident)],
        out_specs=pl.BlockSpec((tm, d), lambda i: (i, 0)),
        out_shape=jax.ShapeDtypeStruct((t, d), F32),
        compiler_params=_cparams(("parallel",)),
        name="out_stage",
    )(x, ya, ym, u, u, gate, norm_post, wa, wm, wo)


def kernel(x_prompt, x_sample, cache_k, cache_v, state_C, state_n, state_m, page_table, c_prompt, c_sample,
           w_ada, b_ada, norm_pre, norm_post, w_in, b_igate, b_fgate, lambda_q1, lambda_k1, lambda_q2,
           lambda_k2, attn_head_norm, mlstm_head_norm, w_br_a, w_br_m, w_out):
    depth = w_in.shape[0]
    assert depth == 1, "single-layer trunk"
    batch, seq, d = x_prompt.shape
    bd, tq, _ = x_sample.shape
    heads_a, e = cache_k.shape[3], cache_k.shape[4]
    hd = e // 2
    d_att = heads_a * e
    heads_m, dv, dk = state_C.shape[2], state_C.shape[3], state_C.shape[4]
    d_mv, d_mqk = heads_m * dv, heads_m * dk
    layer = 0
    lam_init = 0.8 - 0.6 * math.exp(-0.3 * layer)

    names = ("q_a", "k_a", "v_a", "z_a", "q_m", "k_m", "v_m", "o_m", "z_m", "i_m", "f_m", "g_a", "g_m")
    widths = (d_att, d_att, d_att, d_att, d_mqk, d_mqk, d_mv, d_mv, d_mv, heads_m, heads_m, d, d)
    src = dict(zip(names, np.concatenate([[0], np.cumsum(widths)[:-1]]).tolist()))
    main_cols = src["i_m"]
    cols = {nm: src[nm] for nm in names[:9]}
    cols["g_a"], cols["g_m"] = main_cols, main_cols + d

    wt = w_in.reshape(w_in.shape[1:]).T
    inproj = functools.partial(_inproj_call, main_cols=main_cols, gate_cols=2 * heads_m, k_col=src["k_a"],
                               v_col=src["v_a"], kv_cols=d_att, tn=512)
    wa = w_br_a[layer].astype(BF16)
    wm = w_br_m[layer].astype(BF16)
    wo = w_out[layer].astype(BF16)

    n_c = batch + bd
    pad = (-n_c) % 8
    c_all = jnp.concatenate([c_prompt, c_sample, jnp.zeros((pad, d), F32)], axis=0)
    mod = _mod_call(c_all, w_ada[layer], b_ada[layer].reshape(1, -1))
    shift, scale, gate = mod[:, :d], mod[:, d:2 * d], mod[:, 2 * d:]

    lamv = jnp.stack([lambda_q1[layer], lambda_k1[layer], lambda_q2[layer], lambda_k2[layer]])
    gn_a = attn_head_norm[layer].reshape(heads_a, e)
    gn_m = mlstm_head_norm[layer].reshape(heads_m, 1, dv)
    gb = jnp.stack([b_igate[layer], b_fgate[layer]])
    npre = norm_pre[layer].reshape(1, d)
    npost = norm_post[layer].reshape(1, d)

    xp = x_prompt.reshape(batch * seq, d)
    grp = lambda a: a[:batch].reshape(batch, 1, d)
    u_p, k_p, v_p, g_p, vt_p = inproj(xp, grp(scale), grp(shift), npre, wt, rowwise=False,
                                      rows_per_group=seq, tm=1024, u_dtype=BF16, want_vt=True, x_buffers=1)
    ya_p = _attn_prompt_call(u_p, vt_p, lamv, gn_a.reshape(heads_a, e, 1), batch=batch, seq=seq,
                             heads=heads_a, hd=hd, lam_init=lam_init, cols=cols)
    nc = seq // MLSTM_CHUNK
    gates_p = g_p.reshape(batch, nc, MLSTM_CHUNK, 2 * heads_m).transpose(0, 1, 3, 2)
    ym_p, c_p, n_p, m_p = _mlstm_prompt_call(
        u_p, gates_p, gb, gn_m, jnp.zeros((batch, heads_m, dv, dk), F32), jnp.zeros((batch, heads_m, 1, dk), F32),
        jnp.zeros((batch, heads_m, 1, 1), F32), batch=batch, seq=seq, heads=heads_m, dk=dk, dv=dv, cols=cols)
    y_p = _out_call(xp, ya_p, ym_p, u_p, grp(gate), npost, wa, wm, wo, rowwise=False, rows_per_group=seq,
                    tm=512, cols=cols)

    xs = x_sample.reshape(bd * tq, d)
    rep = lambda a: jnp.repeat(a[batch:batch + bd], tq, axis=0)
    u_s, k_s, v_s, g_s = inproj(xs, rep(scale), rep(shift), npre, wt, rowwise=True,
                                rows_per_group=tq, tm=min(512, bd * tq), u_dtype=F32, want_vt=False,
                                x_buffers=2)
    ya_s = _attn_sample_call(u_s, k_s, v_s, cache_k.reshape(cache_k.shape[1:]),
                             cache_v.reshape(cache_v.shape[1:]), page_table, lamv, gn_a,
                             tq=tq, heads=heads_a, hd=hd, lam_init=lam_init, cols=cols)
    gates_s = g_s.reshape(bd, tq, 2, heads_m).transpose(0, 3, 2, 1)
    ym_s, c_s, n_s, m_s = _mlstm_sample_call(
        u_s, gates_s, gb, gn_m, state_C.reshape(state_C.shape[1:]), state_n.reshape(bd, heads_m, 1, dk),
        state_m.reshape(bd, heads_m, 1, 1), tq=tq, heads=heads_m, dk=dk, dv=dv, cols=cols)
    y_s = _out_call(xs, ya_s, ym_s, u_s, rep(gate), npost, wa, wm, wo, rowwise=True, rows_per_group=tq,
                    tm=min(256, bd * tq), cols=cols)

    return (y_p.reshape(batch, seq, d), y_s.reshape(bd, tq, d),
            k_p.reshape(1, batch, seq, heads_a, e), v_p.reshape(1, batch, seq, heads_a, e),
            c_p.reshape(1, batch, heads_m, dv, dk), n_p.reshape(1, batch, heads_m, dk),
            m_p.reshape(1, batch, heads_m),
            k_s.reshape(1, bd, tq, heads_a, e), v_s.reshape(1, bd, tq, heads_a, e),
            c_s.reshape(1, bd, heads_m, dv, dk), n_s.reshape(1, bd, heads_m, dk), m_s.reshape(1, bd, heads_m))
```

```python
import functools
import math

import jax
import jax.numpy as jnp
import numpy as np
from jax import lax
from jax.experimental import pallas as pl
from jax.experimental.pallas import tpu as pltpu

F32 = jnp.float32
BF16 = jnp.bfloat16
EPS = 1e-6
NEG = -1e30
VMEM_LIMIT = 56 * 1024 * 1024
LANE = 128
MLSTM_CHUNK = 256
ATT_BLOCK = 512
ATT_HEADS_PER_STEP = 4
PAGES_PER_STEP = 16
MLSTM_DEC_BB = 16

_NT = (((1,), (1,)), ((), ()))
_TN = (((0,), (0,)), ((), ()))


def _cparams(sem):
    return pltpu.CompilerParams(dimension_semantics=sem, vmem_limit_bytes=VMEM_LIMIT)


def _sigmoid(x):
    return 1.0 / (1.0 + jnp.exp(-x))


def _silu(x):
    return x * _sigmoid(x)


def _mod_kernel(c_ref, w_ref, b_ref, o_ref):
    a = _silu(c_ref[...]).astype(BF16)
    o_ref[...] = jnp.dot(a, w_ref[...].astype(BF16), preferred_element_type=F32) + b_ref[...]


def _mod_call(c_all, w_ada, b_ada):
    rows, d = c_all.shape
    n = w_ada.shape[1]
    tn = 512
    return pl.pallas_call(
        _mod_kernel,
        grid=(n // tn,),
        in_specs=[pl.BlockSpec((rows, d), lambda j: (0, 0)),
                  pl.BlockSpec((d, tn), lambda j: (0, j)),
                  pl.BlockSpec((1, tn), lambda j: (0, j))],
        out_specs=pl.BlockSpec((rows, tn), lambda j: (0, j)),
        out_shape=jax.ShapeDtypeStruct((rows, n), F32),
        compiler_params=_cparams(("parallel",)),
        name="mod",
    )(c_all, w_ada, b_ada)


def _inproj_kernel(x_ref, sc_ref, sh_ref, g_ref, w_ref, wg_ref, u_ref, k_ref, v_ref, gate_ref, *rest,
                   rowwise, kj0, vj0, nkv, want_vt):
    if want_vt:
        vt_ref, h_ref = rest
    else:
        (h_ref,) = rest
    j = pl.program_id(1)

    @pl.when(j == 0)
    def _():
        x = x_ref[...]
        xn = x * lax.rsqrt(jnp.mean(x * x, axis=-1, keepdims=True) + EPS) * g_ref[...]
        sc = sc_ref[...] if rowwise else sc_ref[0]
        sh = sh_ref[...] if rowwise else sh_ref[0]
        h = (xn * (1.0 + sc) + sh).astype(BF16)
        h_ref[...] = h
        gate_ref[...] = lax.dot_general(h, wg_ref[...].astype(BF16), _NT, preferred_element_type=F32)

    res = lax.dot_general(h_ref[...], w_ref[...].astype(BF16), _NT, preferred_element_type=F32)
    u_ref[...] = res.astype(u_ref.dtype)

    @pl.when((j >= kj0) & (j < kj0 + nkv))
    def _():
        k_ref[...] = res

    @pl.when((j >= vj0) & (j < vj0 + nkv))
    def _():
        v_ref[...] = res
        if want_vt:
            vt_ref[...] = res.T.astype(BF16)


def _inproj_call(x, scale, shift, norm_pre, wt, *, rowwise, rows_per_group, tm, tn, u_dtype,
                 main_cols, gate_cols, k_col, v_col, kv_cols, want_vt, x_buffers):
    t, d = x.shape
    n_main = main_cols // tn
    n_tiles = (wt.shape[0] - gate_cols) // tn
    kj0, vj0, nkv = k_col // tn, v_col // tn, kv_cols // tn
    clamp = lambda j, j0: jnp.clip(j - j0, 0, nkv - 1)
    if rowwise:
        mspec = pl.BlockSpec((tm, d), lambda i, j: (i, 0))
    else:
        mspec = pl.BlockSpec((1, 1, d), lambda i, j: ((i * tm) // rows_per_group, 0, 0))
    out_specs = [pl.BlockSpec((tm, tn), lambda i, j: (i, j)),
                 pl.BlockSpec((tm, tn), lambda i, j: (i, clamp(j, kj0))),
                 pl.BlockSpec((tm, tn), lambda i, j: (i, clamp(j, vj0))),
                 pl.BlockSpec((tm, gate_cols), lambda i, j: (i, 0))]
    out_shape = [jax.ShapeDtypeStruct((t, n_tiles * tn), u_dtype),
                 jax.ShapeDtypeStruct((t, kv_cols), F32), jax.ShapeDtypeStruct((t, kv_cols), F32),
                 jax.ShapeDtypeStruct((t, gate_cols), F32)]
    if want_vt:
        out_specs.append(pl.BlockSpec((tn, tm), lambda i, j: (clamp(j, vj0), i)))
        out_shape.append(jax.ShapeDtypeStruct((kv_cols, t), BF16))
    return pl.pallas_call(
        functools.partial(_inproj_kernel, rowwise=rowwise, kj0=kj0, vj0=vj0, nkv=nkv, want_vt=want_vt),
        grid=(t // tm, n_tiles),
        in_specs=[pl.BlockSpec((tm, d), lambda i, j: (i, 0), pipeline_mode=pl.Buffered(x_buffers)),
                  mspec, mspec,
                  pl.BlockSpec((1, d), lambda i, j: (0, 0)),
                  pl.BlockSpec((pl.Element(tn), pl.Element(d)),
                               lambda i, j: (pl.multiple_of(j * tn + jnp.where(j >= n_main, gate_cols, 0), 8), 0)),
                  pl.BlockSpec((gate_cols, d), lambda i, j: (main_cols // gate_cols, 0))],
        out_specs=out_specs,
        out_shape=out_shape,
        scratch_shapes=[pltpu.VMEM((tm, d), BF16)],
        compiler_params=_cparams(("parallel", "arbitrary")),
        name="inproj",
    )(x, scale, shift, norm_pre, wt, wt)


def _lambda_value(lamv_ref, lam_init):
    lv = lamv_ref[...]
    d1 = jnp.sum(lv[0:1] * lv[1:2], axis=-1, keepdims=True)
    d2 = jnp.sum(lv[2:3] * lv[3:4], axis=-1, keepdims=True)
    return jnp.exp(d1) - jnp.exp(d2) + lam_init


def _softmax_step(s, shift, v, m_ref, l_ref, acc_ref):
    m_old = m_ref[...]
    m_new = jnp.maximum(m_old, jnp.max(s, axis=-1, keepdims=True) + shift)
    alpha = jnp.exp(m_old - m_new)
    p = jnp.exp(s - (m_new - shift))
    l_ref[...] = alpha * l_ref[...] + jnp.sum(p, axis=-1, keepdims=True)
    acc_ref[...] = alpha * acc_ref[...] + jnp.dot(p.astype(BF16), v, preferred_element_type=F32)
    m_ref[...] = m_new


def _head_post(att, gain, lam_init, z):
    r = att * lax.rsqrt(jnp.mean(att * att, axis=-1, keepdims=True) + EPS) * gain
    return r * (1.0 - lam_init) * _silu(z)


def _attn_prompt_kernel(qi_ref, kj_ref, q_ref, k_ref, vt_ref, z_ref, bias_ref, slope_ref, lamv_ref,
                        gn_ref, o_ref, qs_ref, m_ref, l_ref, acc_ref, *, blk, hd, hp, lam_init):
    e = 2 * hd
    hg = pl.program_id(1)
    p = pl.program_id(2)
    i = qi_ref[p]
    j = kj_ref[p]
    streams = [(t, c) for t in range(hp) for c in range(2)]

    @pl.when(j == 0)
    def _():
        lane = lax.broadcasted_iota(jnp.int32, (blk, e), 1)
        for t in range(hp):
            q = q_ref[:, t * e:(t + 1) * e].astype(F32) * (hd ** -0.5)
            qs_ref[t, 0] = jnp.where(lane < hd, q, 0.0).astype(BF16)
            qs_ref[t, 1] = jnp.where(lane >= hd, q, 0.0).astype(BF16)
        m_ref[...] = jnp.full(m_ref.shape, NEG, F32)
        l_ref[...] = jnp.zeros(l_ref.shape, F32)
        acc_ref[...] = jnp.zeros(acc_ref.shape, F32)

    diag = (j == i).astype(jnp.int32)
    rel = ((j - i) * blk).astype(F32)
    shifts = [slope_ref[hg * hp + t] * rel for t in range(hp)]
    ks = [k_ref[:, t * e:(t + 1) * e] for t in range(hp)]
    logits = [lax.dot_general(ks[t], qs_ref[t, c], _NT, preferred_element_type=F32) + bias_ref[t, diag]
              for (t, c) in streams]
    probs = []
    for (t, c), s in zip(streams, logits):
        m_old = m_ref[t, c]
        m_new = jnp.maximum(m_old, jnp.max(s, axis=0, keepdims=True) + shifts[t])
        alpha = jnp.exp(m_old - m_new)
        pr = jnp.exp(s - (m_new - shifts[t]))
        l_ref[t, c] = alpha * l_ref[t, c] + jnp.sum(pr, axis=0, keepdims=True)
        m_ref[t, c] = m_new
        probs.append((alpha, pr.astype(BF16)))
    for (t, c), (alpha, pr) in zip(streams, probs):
        acc_ref[t, c] = alpha * acc_ref[t, c] + jnp.dot(vt_ref[t * e:(t + 1) * e, :], pr,
                                                        preferred_element_type=F32)

    @pl.when(j == i)
    def _():
        lam = _lambda_value(lamv_ref, lam_init)
        for t in range(hp):
            att = acc_ref[t, 0] / l_ref[t, 0] - lam * (acc_ref[t, 1] / l_ref[t, 1])
            r = att * lax.rsqrt(jnp.mean(att * att, axis=0, keepdims=True) + EPS) * gn_ref[t]
            z = z_ref[:, t * e:(t + 1) * e].astype(F32)
            o_ref[:, t * e:(t + 1) * e] = (r.T * (1.0 - lam_init) * _silu(z)).astype(o_ref.dtype)


def _attn_prompt_call(u, vt, lamv, gn, *, batch, seq, heads, hd, lam_init, cols):
    blk = ATT_BLOCK
    nb = seq // blk
    qi = np.array([i for i in range(nb) for _ in range(i + 1)], np.int32)
    kj = np.array([j for i in range(nb) for j in range(i + 1)], np.int32)
    hp = ATT_HEADS_PER_STEP
    assert heads % hp == 0, (heads, hp)
    slopes = 2.0 ** (-8.0 * np.arange(1, heads + 1, dtype=np.float32) / heads)
    rel = (np.arange(blk)[:, None] - np.arange(blk)[None, :]).astype(np.float32)
    off = slopes[:, None, None] * rel[None]
    dia = np.where(rel[None] <= 0, off, NEG).astype(np.float32)
    bias = jnp.asarray(np.stack([off, dia], axis=1))
    e = 2 * hd
    w = hp * e
    qb, kb, zb = cols["q_a"] // w, cols["k_a"] // w, cols["z_a"] // w

    grid_spec = pltpu.PrefetchScalarGridSpec(
        num_scalar_prefetch=2,
        grid=(batch, heads // hp, len(qi)),
        in_specs=[
            pl.BlockSpec((blk, w), lambda b, h, p, qi, kj: (b * nb + qi[p], qb + h)),
            pl.BlockSpec((blk, w), lambda b, h, p, qi, kj: (b * nb + kj[p], kb + h)),
            pl.BlockSpec((w, blk), lambda b, h, p, qi, kj: (h, b * nb + kj[p])),
            pl.BlockSpec((blk, w), lambda b, h, p, qi, kj: (b * nb + qi[p], zb + h)),
            pl.BlockSpec((hp, 2, blk, blk), lambda b, h, p, qi, kj: (h, 0, 0, 0)),
            pl.BlockSpec(memory_space=pltpu.SMEM),
            pl.BlockSpec(lamv.shape, lambda b, h, p, qi, kj: (0, 0)),
            pl.BlockSpec((hp, e, 1), lambda b, h, p, qi, kj: (h, 0, 0)),
        ],
        out_specs=pl.BlockSpec((blk, w), lambda b, h, p, qi, kj: (b * nb + qi[p], h)),
        scratch_shapes=[pltpu.VMEM((hp, 2, blk, e), BF16), pltpu.VMEM((hp, 2, 1, blk), F32),
                        pltpu.VMEM((hp, 2, 1, blk), F32), pltpu.VMEM((hp, 2, e, blk), F32)],
    )
    return pl.pallas_call(
        functools.partial(_attn_prompt_kernel, blk=blk, hd=hd, hp=hp, lam_init=lam_init),
        grid_spec=grid_spec,
        out_shape=jax.ShapeDtypeStruct((batch * seq, heads * e), BF16),
        compiler_params=_cparams(("parallel", "parallel", "arbitrary")),
        name="attn_prompt",
    )(jnp.asarray(qi), jnp.asarray(kj), u, u, vt, u, bias, jnp.asarray(slopes), lamv, gn)


def _attn_sample_kernel(pt_ref, q_ref, z_ref, kn_ref, vn_ref, *rest, pps, heads, hd, tq, page, lam_init):
    kp_refs = rest[:pps]
    vp_refs = rest[pps:2 * pps]
    (bp_ref, bn_ref, slope_ref, lamv_ref, gn_ref, o_ref, qb_ref, s_ref, m_ref, l_ref, acc_ref) = rest[2 * pps:]
    j = pl.program_id(1)
    e = 2 * hd
    nr = heads * tq
    pw = page * heads

    @pl.when(j == 0)
    def _():
        q = q_ref[...].astype(F32) * (hd ** -0.5)
        lane = lax.broadcasted_iota(jnp.int32, (tq, e), 1)
        for hh in range(heads):
            qh = q[:, hh * e:(hh + 1) * e]
            qb_ref[hh * tq:(hh + 1) * tq, :] = jnp.where(lane < hd, qh, 0.0)
            qb_ref[nr + hh * tq:nr + (hh + 1) * tq, :] = jnp.where(lane >= hd, qh, 0.0)
        m_ref[...] = jnp.full(m_ref.shape, NEG, F32)
        l_ref[...] = jnp.zeros(l_ref.shape, F32)
        acc_ref[...] = jnp.zeros(acc_ref.shape, F32)

    qb = qb_ref[...].astype(BF16)
    shifts = [slope_ref[...] * ((j * pps + s) * page).astype(F32) for s in range(pps)]
    mx = None
    for s in range(pps):
        kp = kp_refs[s][...].reshape(pw, e).astype(BF16)
        sc = lax.dot_general(qb, kp, _NT, preferred_element_type=F32) + bp_ref[...]
        s_ref[:, s * pw:(s + 1) * pw] = sc
        ms = jnp.max(sc, axis=-1, keepdims=True) + shifts[s]
        mx = ms if mx is None else jnp.maximum(mx, ms)
    m_old = m_ref[...]
    m_new = jnp.maximum(m_old, mx)
    alpha = jnp.exp(m_old - m_new)
    lsum = jnp.zeros_like(m_old)
    pv = jnp.zeros(acc_ref.shape, F32)
    for s in range(pps):
        pr = jnp.exp(s_ref[:, s * pw:(s + 1) * pw] - (m_new - shifts[s]))
        lsum = lsum + jnp.sum(pr, axis=-1, keepdims=True)
        vp = vp_refs[s][...].reshape(pw, e).astype(BF16)
        pv = pv + jnp.dot(pr.astype(BF16), vp, preferred_element_type=F32)
    l_ref[...] = alpha * l_ref[...] + lsum
    acc_ref[...] = alpha * acc_ref[...] + pv
    m_ref[...] = m_new

    @pl.when(j == pl.num_programs(1) - 1)
    def _():
        kn = kn_ref[...]
        vn = vn_ref[...]
        knr = jnp.concatenate([kn[:, hh * e:(hh + 1) * e] for hh in range(heads)], axis=0).astype(BF16)
        vnr = jnp.concatenate([vn[:, hh * e:(hh + 1) * e] for hh in range(heads)], axis=0).astype(BF16)
        sc = lax.dot_general(qb, knr, _NT, preferred_element_type=F32) + bn_ref[...]
        _softmax_step(sc, 0.0, vnr, m_ref, l_ref, acc_ref)
        o = acc_ref[...] / l_ref[...]
        lam = _lambda_value(lamv_ref, lam_init)
        att = o[0:nr] - lam * o[nr:2 * nr]
        z = z_ref[...].astype(F32)
        for hh in range(heads):
            r = _head_post(att[hh * tq:(hh + 1) * tq], gn_ref[hh:hh + 1, :], lam_init,
                           z[:, hh * e:(hh + 1) * e])
            o_ref[:, hh * e:(hh + 1) * e] = r.astype(o_ref.dtype)


def _attn_sample_call(u, k_new, v_new, cache_k, cache_v, page_table, lamv, gn, *, tq, heads, hd,
                      lam_init, cols):
    bd, npg = page_table.shape
    page = cache_k.shape[1]
    pps = PAGES_PER_STEP
    e = 2 * hd
    nr = heads * tq
    past = npg * page
    slopes = 2.0 ** (-8.0 * np.arange(1, heads + 1, dtype=np.float32) / heads)
    r_head = (np.arange(2 * nr) % nr) // tq
    r_tok = np.arange(2 * nr) % tq
    r_slope = slopes[r_head]
    c_tok, c_head = np.arange(page * heads) // heads, np.arange(page * heads) % heads
    bp = r_slope[:, None] * (c_tok[None, :] - (past + r_tok)[:, None])
    bp = np.where(r_head[:, None] == c_head[None, :], bp, NEG).astype(np.float32)
    n_head, n_tok = np.arange(nr) // tq, np.arange(nr) % tq
    bn = r_slope[:, None] * (n_tok[None, :] - r_tok[:, None])
    ok = (r_head[:, None] == n_head[None, :]) & (n_tok[None, :] <= r_tok[:, None])
    bn = np.where(ok, bn, NEG).astype(np.float32)
    slope_col = r_slope.reshape(2 * nr, 1).astype(np.float32)
    w = heads * e

    def page_spec(s):
        return pl.BlockSpec((None, page, heads, e),
                            lambda b, j, pt: (pt[b * npg + j * pps + s], 0, 0, 0))

    const2 = lambda b, j, pt: (0, 0)
    grid_spec = pltpu.PrefetchScalarGridSpec(
        num_scalar_prefetch=1,
        grid=(bd, npg // pps),
        in_specs=[pl.BlockSpec((tq, w), lambda b, j, pt: (b, cols["q_a"] // w)),
                  pl.BlockSpec((tq, w), lambda b, j, pt: (b, cols["z_a"] // w)),
                  pl.BlockSpec((tq, w), lambda b, j, pt: (b, 0)),
                  pl.BlockSpec((tq, w), lambda b, j, pt: (b, 0))]
                 + [page_spec(s) for s in range(pps)] + [page_spec(s) for s in range(pps)]
                 + [pl.BlockSpec(bp.shape, const2), pl.BlockSpec(bn.shape, const2),
                    pl.BlockSpec(slope_col.shape, const2), pl.BlockSpec(lamv.shape, const2),
                    pl.BlockSpec(gn.shape, const2)],
        out_specs=pl.BlockSpec((tq, w), lambda b, j, pt: (b, 0)),
        scratch_shapes=[pltpu.VMEM((2 * nr, e), F32), pltpu.VMEM((2 * nr, pps * page * heads), F32),
                        pltpu.VMEM((2 * nr, 1), F32), pltpu.VMEM((2 * nr, 1), F32),
                        pltpu.VMEM((2 * nr, e), F32)],
    )
    return pl.pallas_call(
        functools.partial(_attn_sample_kernel, pps=pps, heads=heads, hd=hd, tq=tq, page=page,
                          lam_init=lam_init),
        grid_spec=grid_spec,
        out_shape=jax.ShapeDtypeStruct((bd * tq, w), F32),
        compiler_params=_cparams(("parallel", "arbitrary")),
        name="attn_sample",
    )(page_table.reshape(-1), u, u, k_new, v_new, *([cache_k] * pps), *([cache_v] * pps),
      jnp.asarray(bp), jnp.asarray(bn), jnp.asarray(slope_col), lamv, gn)


def _mlstm_chunks(chains, scale):
    ln = chains[0][0].shape[0]
    row = lax.broadcasted_iota(jnp.int32, (ln, ln), 0)
    col = lax.broadcasted_iota(jnp.int32, (ln, ln), 1)
    tri = col <= row
    eye = col == row

    gate = []
    for (q, k, v, ig_row, lf_row, c_st, n_st, m_st) in chains:
        lf_b = jnp.broadcast_to(lf_row, (ln, ln))
        ig_b = jnp.broadcast_to(ig_row, (ln, ln))
        b_col = jnp.sum(jnp.where(tri, lf_b, 0.0), axis=1, keepdims=True)
        lf_col = jnp.sum(jnp.where(eye, lf_b, 0.0), axis=1, keepdims=True)
        ig_col = jnp.sum(jnp.where(eye, ig_b, 0.0), axis=1, keepdims=True)
        b_row = jnp.sum(jnp.where(row <= col, jnp.broadcast_to(lf_col, (ln, ln)), 0.0),
                        axis=0, keepdims=True)
        dmat = jnp.where(tri, b_col - b_row + ig_row, NEG)
        a_col = b_col + m_st
        m_t = jnp.maximum(a_col, jnp.max(dmat, axis=1, keepdims=True))
        m_new = m_t[ln - 1:ln, :]
        b_last = b_col[ln - 1:ln, :]
        wk = jnp.exp(b_last - b_col + ig_col - m_new)
        decay = jnp.exp(b_last + m_st - m_new)
        gate.append((jnp.exp(dmat - m_t), jnp.exp(a_col - m_t), m_t, m_new, wk, decay))

    first = []
    for (q, k, v, _, _, c_st, _, _), (_, _, _, _, wk, _) in zip(chains, gate):
        qk = lax.dot_general(q, k, _NT, preferred_element_type=F32)
        q_c = lax.dot_general(q, c_st.astype(BF16), _NT, preferred_element_type=F32)
        vw = (v.astype(F32) * wk).astype(BF16)
        vk = lax.dot_general(vw, k, _TN, preferred_element_type=F32)
        first.append((qk, q_c, vk))

    out = []
    for (q, k, v, _, _, c_st, n_st, _), (dexp, inter, m_t, m_new, wk, decay), (qk, q_c, vk) in zip(
            chains, gate, first):
        s = qk * scale * dexp
        num = inter * q_c + jnp.dot(s.astype(BF16), v, preferred_element_type=F32)
        den = (inter * jnp.sum(q.astype(F32) * n_st, axis=1, keepdims=True)
               + jnp.sum(s, axis=1, keepdims=True))
        h = num / jnp.maximum(jnp.abs(den), jnp.exp(-m_t))
        c_new = decay * c_st + vk * scale
        n_new = decay * n_st + jnp.sum(k.astype(F32) * wk, axis=0, keepdims=True) * scale
        out.append((h, c_new, n_new, m_new))
    return out


def _log_sigmoid(x):
    return jnp.minimum(x, 0.0) - jnp.log(1.0 + jnp.exp(-jnp.abs(x)))


def _mlstm_post(h, gain, o, z):
    hn = h * lax.rsqrt(jnp.mean(h * h, axis=-1, keepdims=True) + EPS) * gain
    return _sigmoid(o) * hn * _silu(z)


def _mlstm_prompt_kernel(gb_ref, q_ref, k_ref, v_ref, o_ref, z_ref, g_ref, gn_ref, c0_ref, n0_ref, m0_ref,
                         y_ref, c_out, n_out, m_out, c_s, n_s, m_s, *, scale, heads, dk, dv):
    ci = pl.program_id(1)

    @pl.when(ci == 0)
    def _():
        c_s[...] = c0_ref[...]
        n_s[...] = n0_ref[...]
        m_s[...] = m0_ref[...]

    g = g_ref[...]
    chains = []
    for hh in range(heads):
        ks = slice(hh * dk, (hh + 1) * dk)
        vs = slice(hh * dv, (hh + 1) * dv)
        ig = g[hh:hh + 1, :] + gb_ref[0, hh]
        lf = _log_sigmoid(g[heads + hh:heads + hh + 1, :] + gb_ref[1, hh])
        chains.append((q_ref[:, ks], k_ref[:, ks], v_ref[:, vs], ig, lf, c_s[hh], n_s[hh], m_s[hh]))
    for hh, (h, c_new, n_new, m_new) in enumerate(_mlstm_chunks(chains, scale)):
        vs = slice(hh * dv, (hh + 1) * dv)
        c_s[hh] = c_new
        n_s[hh] = n_new
        m_s[hh] = m_new
        y_ref[:, vs] = _mlstm_post(h, gn_ref[hh], o_ref[:, vs].astype(F32),
                                   z_ref[:, vs].astype(F32)).astype(y_ref.dtype)

    @pl.when(ci == pl.num_programs(1) - 1)
    def _():
        c_out[...] = c_s[...]
        n_out[...] = n_s[...]
        m_out[...] = m_s[...]


def _mlstm_prompt_call(u, gates, gb, gn, c0, n0, m0, *, batch, seq, heads, dk, dv, cols):
    ln = MLSTM_CHUNK
    nc = seq // ln
    wk, wv = heads * dk, heads * dv
    qb, kb, vb, ob, zb = (cols["q_m"] // wk, cols["k_m"] // wk, cols["v_m"] // wv, cols["o_m"] // wv,
                          cols["z_m"] // wv)
    st4 = lambda b, c: (b, 0, 0, 0)
    return pl.pallas_call(
        functools.partial(_mlstm_prompt_kernel, scale=dk ** -0.5, heads=heads, dk=dk, dv=dv),
        grid=(batch, nc),
        in_specs=[pl.BlockSpec(memory_space=pltpu.SMEM),
                  pl.BlockSpec((ln, wk), lambda b, c: (b * nc + c, qb)),
                  pl.BlockSpec((ln, wk), lambda b, c: (b * nc + c, kb)),
                  pl.BlockSpec((ln, wv), lambda b, c: (b * nc + c, vb)),
                  pl.BlockSpec((ln, wv), lambda b, c: (b * nc + c, ob)),
                  pl.BlockSpec((ln, wv), lambda b, c: (b * nc + c, zb)),
                  pl.BlockSpec((None, None, 2 * heads, ln), lambda b, c: (b, c, 0, 0)),
                  pl.BlockSpec((heads, 1, dv), lambda b, c: (0, 0, 0)),
                  pl.BlockSpec((None, heads, dv, dk), st4),
                  pl.BlockSpec((None, heads, 1, dk), st4),
                  pl.BlockSpec((None, heads, 1, 1), st4)],
        out_specs=[pl.BlockSpec((ln, wv), lambda b, c: (b * nc + c, 0)),
                   pl.BlockSpec((None, heads, dv, dk), st4),
                   pl.BlockSpec((None, heads, 1, dk), st4),
                   pl.BlockSpec((None, heads, 1, 1), st4)],
        out_shape=[jax.ShapeDtypeStruct((batch * seq, wv), BF16),
                   jax.ShapeDtypeStruct((batch, heads, dv, dk), F32),
                   jax.ShapeDtypeStruct((batch, heads, 1, dk), F32),
                   jax.ShapeDtypeStruct((batch, heads, 1, 1), F32)],
        scratch_shapes=[pltpu.VMEM((heads, dv, dk), F32), pltpu.VMEM((heads, 1, dk), F32),
                        pltpu.VMEM((heads, 1, 1), F32)],
        compiler_params=_cparams(("parallel", "arbitrary")),
        name="mlstm_prompt",
    )(gb, u, u, u, u, u, gates, gn, c0, n0, m0)


def _mlstm_sample_kernel(gb_ref, q_ref, k_ref, v_ref, o_ref, z_ref, g_ref, gn_ref, c0_ref, n0_ref, m0_ref,
                         y_ref, c_out, n_out, m_out, *, scale, bb, tq):
    hh = pl.program_id(1)
    q = q_ref[...].astype(F32)
    k = k_ref[...].astype(F32)
    v = v_ref[...].astype(F32)
    o = o_ref[...].astype(F32)
    z = z_ref[...].astype(F32)
    gain = gn_ref[...]
    chains = []
    for i in range(bb):
        rs = slice(i * tq, (i + 1) * tq)
        g = g_ref[i]
        ig = g[0:1, :] + gb_ref[0, hh]
        lf = _log_sigmoid(g[1:2, :] + gb_ref[1, hh])
        chains.append((q[rs].astype(BF16), k[rs].astype(BF16), v[rs].astype(BF16), ig, lf,
                       c0_ref[i], n0_ref[i], m0_ref[i]))
    for i, (h, c_new, n_new, m_new) in enumerate(_mlstm_chunks(chains, scale)):
        rs = slice(i * tq, (i + 1) * tq)
        y_ref[rs, :] = _mlstm_post(h, gain, o[rs], z[rs]).astype(y_ref.dtype)
        c_out[i] = c_new
        n_out[i] = n_new
        m_out[i] = m_new


def _mlstm_sample_call(u, gates, gb, gn, c0, n0, m0, *, tq, heads, dk, dv, cols):
    bd = c0.shape[0]
    bb = MLSTM_DEC_BB
    assert bd % bb == 0, (bd, bb)
    qb, kb, vb, ob, zb = (cols["q_m"] // dk, cols["k_m"] // dk, cols["v_m"] // dv, cols["o_m"] // dv,
                          cols["z_m"] // dv)
    st4 = lambda g, h: (g, h, 0, 0)
    return pl.pallas_call(
        functools.partial(_mlstm_sample_kernel, scale=dk ** -0.5, bb=bb, tq=tq),
        grid=(bd // bb, heads),
        in_specs=[pl.BlockSpec(memory_space=pltpu.SMEM),
                  pl.BlockSpec((bb * tq, dk), lambda g, h: (g, qb + h)),
                  pl.BlockSpec((bb * tq, dk), lambda g, h: (g, kb + h)),
                  pl.BlockSpec((bb * tq, dv), lambda g, h: (g, vb + h)),
                  pl.BlockSpec((bb * tq, dv), lambda g, h: (g, ob + h)),
                  pl.BlockSpec((bb * tq, dv), lambda g, h: (g, zb + h)),
                  pl.BlockSpec((bb, None, 2, tq), st4),
                  pl.BlockSpec((None, 1, dv), lambda g, h: (h, 0, 0)),
                  pl.BlockSpec((bb, None, dv, dk), st4),
                  pl.BlockSpec((bb, None, 1, dk), st4),
                  pl.BlockSpec((bb, None, 1, 1), st4)],
        out_specs=[pl.BlockSpec((bb * tq, dv), lambda g, h: (g, h)),
                   pl.BlockSpec((bb, None, dv, dk), st4),
                   pl.BlockSpec((bb, None, 1, dk), st4),
                   pl.BlockSpec((bb, None, 1, 1), st4)],
        out_shape=[jax.ShapeDtypeStruct((bd * tq, heads * dv), F32),
                   jax.ShapeDtypeStruct((bd, heads, dv, dk), F32),
                   jax.ShapeDtypeStruct((bd, heads, 1, dk), F32),
                   jax.ShapeDtypeStruct((bd, heads, 1, 1), F32)],
        compiler_params=_cparams(("parallel", "parallel")),
        name="mlstm_sample",
    )(gb, u, u, u, u, u, gates, gn, c0, n0, m0)


def _out_kernel(x_ref, ya_ref, ym_ref, ga_ref, gm_ref, gate_ref, np_ref, wa_ref, wm_ref, wo_ref, o_ref,
                *, rowwise):
    ya = jnp.dot(ya_ref[...].astype(BF16), wa_ref[...], preferred_element_type=F32)
    ym = jnp.dot(ym_ref[...].astype(BF16), wm_ref[...], preferred_element_type=F32)
    y2 = _sigmoid(ga_ref[...].astype(F32)) * ya + _sigmoid(gm_ref[...].astype(F32)) * ym
    y = jnp.dot(y2.astype(BF16), wo_ref[...], preferred_element_type=F32)
    yn = y * lax.rsqrt(jnp.mean(y * y, axis=-1, keepdims=True) + EPS) * np_ref[...]
    gate = gate_ref[...] if rowwise else gate_ref[0]
    o_ref[...] = x_ref[...] + gate * yn


def _out_call(x, ya, ym, u, gate, norm_post, wa, wm, wo, *, rowwise, rows_per_group, tm, cols):
    t, d = x.shape
    da = ya.shape[1]
    ga_blk, gm_blk = cols["g_a"] // d, cols["g_m"] // d
    if rowwise:
        gspec = pl.BlockSpec((tm, d), lambda i: (i, 0))
    else:
        gspec = pl.BlockSpec((1, 1, d), lambda i: ((i * tm) // rows_per_group, 0, 0))
    resident = dict(pipeline_mode=pl.Buffered(1))
    return pl.pallas_call(
        functools.partial(_out_kernel, rowwise=rowwise),
        grid=(t // tm,),
        in_specs=[pl.BlockSpec((tm, d), lambda i: (i, 0)),
                  pl.BlockSpec((tm, da), lambda i: (i, 0)),
                  pl.BlockSpec((tm, da), lambda i: (i, 0)),
                  pl.BlockSpec((tm, d), lambda i: (i, ga_blk)),
                  pl.BlockSpec((tm, d), lambda i: (i, gm_blk)),
                  gspec,
                  pl.BlockSpec((1, d), lambda i: (0, 0)),
                  pl.BlockSpec(wa.shape, lambda i: (0, 0), **resident),
                  pl.BlockSpec(wm.shape, lambda i: (0, 0), **resident),
                  pl.BlockSpec(wo.shape, lambda i: (0, 0), **resident)],
        out_specs=pl.BlockSpec((tm, d), lambda i: (i, 0)),
        out_shape=jax.ShapeDtypeStruct((t, d), F32),
        compiler_params=_cparams(("parallel",)),
        name="out_stage",
    )(x, ya, ym, u, u, gate, norm_post, wa, wm, wo)


def kernel(x_prompt, x_sample, cache_k, cache_v, state_C, state_n, state_m, page_table, c_prompt, c_sample,
           w_ada, b_ada, norm_pre, norm_post, w_in, b_igate, b_fgate, lambda_q1, lambda_k1, lambda_q2,
           lambda_k2, attn_head_norm, mlstm_head_norm, w_br_a, w_br_m, w_out):
    depth = w_in.shape[0]
    assert depth == 1, "single-layer trunk"
    batch, seq, d = x_prompt.shape
    bd, tq, _ = x_sample.shape
    heads_a, e = cache_k.shape[3], cache_k.shape[4]
    hd = e // 2
    d_att = heads_a * e
    heads_m, dv, dk = state_C.shape[2], state_C.shape[3], state_C.shape[4]
    d_mv, d_mqk = heads_m * dv, heads_m * dk
    layer = 0
    lam_init = 0.8 - 0.6 * math.exp(-0.3 * layer)

    names = ("q_a", "k_a", "v_a", "z_a", "q_m", "k_m", "v_m", "o_m", "z_m", "i_m", "f_m", "g_a", "g_m")
    widths = (d_att, d_att, d_att, d_att, d_mqk, d_mqk, d_mv, d_mv, d_mv, heads_m, heads_m, d, d)
    src = dict(zip(names, np.concatenate([[0], np.cumsum(widths)[:-1]]).tolist()))
    main_cols = src["i_m"]
    cols = {nm: src[nm] for nm in names[:9]}
    cols["g_a"], cols["g_m"] = main_cols, main_cols + d

    wt = w_in.reshape(w_in.shape[1:]).T
    inproj = functools.partial(_inproj_call, main_cols=main_cols, gate_cols=2 * heads_m, k_col=src["k_a"],
                               v_col=src["v_a"], kv_cols=d_att, tn=512)
    wa = w_br_a[layer].astype(BF16)
    wm = w_br_m[layer].astype(BF16)
    wo = w_out[layer].astype(BF16)

    n_c = batch + bd
    pad = (-n_c) % 8
    c_all = jnp.concatenate([c_prompt, c_sample, jnp.zeros((pad, d), F32)], axis=0)
    mod = _mod_call(c_all, w_ada[layer], b_ada[layer].reshape(1, -1))
    shift, scale, gate = mod[:, :d], mod[:, d:2 * d], mod[:, 2 * d:]

    lamv = jnp.stack([lambda_q1[layer], lambda_k1[layer], lambda_q2[layer], lambda_k2[layer]])
    gn_a = attn_head_norm[layer].reshape(heads_a, e)
    gn_m = mlstm_head_norm[layer].reshape(heads_m, 1, dv)
    gb = jnp.stack([b_igate[layer], b_fgate[layer]])
    npre = norm_pre[layer].reshape(1, d)
    npost = norm_post[layer].reshape(1, d)

    xp = x_prompt.reshape(batch * seq, d)
    grp = lambda a: a[:batch].reshape(batch, 1, d)
    u_p, k_p, v_p, g_p, vt_p = inproj(xp, grp(scale), grp(shift), npre, wt, rowwise=False,
                                      rows_per_group=seq, tm=1024, u_dtype=BF16, want_vt=True, x_buffers=1)
    ya_p = _attn_prompt_call(u_p, vt_p, lamv, gn_a.reshape(heads_a, e, 1), batch=batch, seq=seq,
                             heads=heads_a, hd=hd, lam_init=lam_init, cols=cols)
    nc = seq // MLSTM_CHUNK
    gates_p = g_p.reshape(batch, nc, MLSTM_CHUNK, 2 * heads_m).transpose(0, 1, 3, 2)
    ym_p, c_p, n_p, m_p = _mlstm_prompt_call(
        u_p, gates_p, gb, gn_m, jnp.zeros((batch, heads_m, dv, dk), F32), jnp.zeros((batch, heads_m, 1, dk), F32),
        jnp.zeros((batch, heads_m, 1, 1), F32), batch=batch, seq=seq, heads=heads_m, dk=dk, dv=dv, cols=cols)
    y_p = _out_call(xp, ya_p, ym_p, u_p, grp(gate), npost, wa, wm, wo, rowwise=False, rows_per_group=seq,
                    tm=512, cols=cols)

    xs = x_sample.reshape(bd * tq, d)
    rep = lambda a: jnp.repeat(a[batch:batch + bd], tq, axis=0)
    u_s, k_s, v_s, g_s = inproj(xs, rep(scale), rep(shift), npre, wt, rowwise=True,
                                rows_per_group=tq, tm=min(512, bd * tq), u_dtype=F32, want_vt=False,
                                x_buffers=2)
    ya_s = _attn_sample_call(u_s, k_s, v_s, cache_k.reshape(cache_k.shape[1:]),
                             cache_v.reshape(cache_v.shape[1:]), page_table, lamv, gn_a,
                             tq=tq, heads=heads_a, hd=hd, lam_init=lam_init, cols=cols)
    gates_s = g_s.reshape(bd, tq, 2, heads_m).transpose(0, 3, 2, 1)
    ym_s, c_s, n_s, m_s = _mlstm_sample_call(
        u_s, gates_s, gb, gn_m, state_C.reshape(state_C.shape[1:]), state_n.reshape(bd, heads_m, 1, dk),
        state_m.reshape(bd, heads_m, 1, 1), tq=tq, heads=heads_m, dk=dk, dv=dv, cols=cols)
    y_s = _out_call(xs, ya_s, ym_s, u_s, rep(gate), npost, wa, wm, wo, rowwise=True, rows_per_group=tq,
                    tm=min(256, bd * tq), cols=cols)

    return (y_p.reshape(batch, seq, d), y_s.reshape(bd, tq, d),
            k_p.reshape(1, batch, seq, heads_a, e), v_p.reshape(1, batch, seq, heads_a, e),
            c_p.reshape(1, batch, heads_m, dv, dk), n_p.reshape(1, batch, heads_m, dk),
            m_p.reshape(1, batch, heads_m),
            k_s.reshape(1, bd, tq, heads_a, e), v_s.reshape(1, bd, tq, heads_a, e),
            c_s.reshape(1, bd, heads_m, dv, dk), n_s.reshape(1, bd, heads_m, dk), m_s.reshape(1, bd, heads_m))
```

```python
import functools
import math

import jax
import jax.numpy as jnp
import numpy as np
from jax import lax
from jax.experimental import pallas as pl
from jax.experimental.pallas import tpu as pltpu

F32 = jnp.float32
BF16 = jnp.bfloat16
EPS = 1e-6
NEG = -1e30
VMEM_LIMIT = 56 * 1024 * 1024
LANE = 128
MLSTM_CHUNK = 256
ATT_BLOCK = 512
ATT_HEADS_PER_STEP = 4
PAGES_PER_STEP = 16
MLSTM_DEC_BB = 16

_NT = (((1,), (1,)), ((), ()))
_TN = (((0,), (0,)), ((), ()))


def _cparams(sem):
    return pltpu.CompilerParams(dimension_semantics=sem, vmem_limit_bytes=VMEM_LIMIT)


def _sigmoid(x):
    return 1.0 / (1.0 + jnp.exp(-x))


def _silu(x):
    return x * _sigmoid(x)


def _mod_kernel(c_ref, w_ref, b_ref, o_ref):
    a = _silu(c_ref[...]).astype(BF16)
    o_ref[...] = jnp.dot(a, w_ref[...].astype(BF16), preferred_element_type=F32) + b_ref[...]


def _mod_call(c_all, w_ada, b_ada):
    rows, d = c_all.shape
    n = w_ada.shape[1]
    tn = 512
    return pl.pallas_call(
        _mod_kernel,
        grid=(n // tn,),
        in_specs=[pl.BlockSpec((rows, d), lambda j: (0, 0)),
                  pl.BlockSpec((d, tn), lambda j: (0, j)),
                  pl.BlockSpec((1, tn), lambda j: (0, j))],
        out_specs=pl.BlockSpec((rows, tn), lambda j: (0, j)),
        out_shape=jax.ShapeDtypeStruct((rows, n), F32),
        compiler_params=_cparams(("parallel",)),
        name="mod",
    )(c_all, w_ada, b_ada)


def _inproj_kernel(x_ref, sc_ref, sh_ref, g_ref, w_ref, wg_ref, u_ref, k_ref, v_ref, gate_ref, *rest,
                   rowwise, kj0, vj0, nkv, want_vt):
    if want_vt:
        vt_ref, h_ref = rest
    else:
        (h_ref,) = rest
    j = pl.program_id(1)

    @pl.when(j == 0)
    def _():
        x = x_ref[...]
        xn = x * lax.rsqrt(jnp.mean(x * x, axis=-1, keepdims=True) + EPS) * g_ref[...]
        if rowwise:
            g, d = sc_ref.shape
            xg = xn.reshape(g, xn.shape[0] // g, d)
            h = (xg * (1.0 + sc_ref[...][:, None, :]) + sh_ref[...][:, None, :]).reshape(xn.shape).astype(BF16)
        else:
            h = (xn * (1.0 + sc_ref[0]) + sh_ref[0]).astype(BF16)
        h_ref[...] = h
        gate_ref[...] = lax.dot_general(h, wg_ref[...].astype(BF16), _NT, preferred_element_type=F32)

    res = lax.dot_general(h_ref[...], w_ref[...].astype(BF16), _NT, preferred_element_type=F32)
    u_ref[...] = res.astype(u_ref.dtype)

    @pl.when((j >= kj0) & (j < kj0 + nkv))
    def _():
        k_ref[...] = res

    @pl.when((j >= vj0) & (j < vj0 + nkv))
    def _():
        v_ref[...] = res
        if want_vt:
            vt_ref[...] = res.T.astype(BF16)


def _inproj_call(x, scale, shift, norm_pre, wt, *, rowwise, rows_per_group, tm, tn, u_dtype,
                 main_cols, gate_cols, k_col, v_col, kv_cols, want_vt, x_buffers):
    t, d = x.shape
    n_main = main_cols // tn
    n_tiles = (wt.shape[0] - gate_cols) // tn
    kj0, vj0, nkv = k_col // tn, v_col // tn, kv_cols // tn
    clamp = lambda j, j0: jnp.clip(j - j0, 0, nkv - 1)
    if rowwise:
        mspec = pl.BlockSpec((tm // rows_per_group, d), lambda i, j: (i, 0))
    else:
        mspec = pl.BlockSpec((1, 1, d), lambda i, j: ((i * tm) // rows_per_group, 0, 0))
    out_specs = [pl.BlockSpec((tm, tn), lambda i, j: (i, j)),
                 pl.BlockSpec((tm, tn), lambda i, j: (i, clamp(j, kj0))),
                 pl.BlockSpec((tm, tn), lambda i, j: (i, clamp(j, vj0))),
                 pl.BlockSpec((tm, gate_cols), lambda i, j: (i, 0))]
    out_shape = [jax.ShapeDtypeStruct((t, n_tiles * tn), u_dtype),
                 jax.ShapeDtypeStruct((t, kv_cols), F32), jax.ShapeDtypeStruct((t, kv_cols), F32),
                 jax.ShapeDtypeStruct((t, gate_cols), F32)]
    if want_vt:
        out_specs.append(pl.BlockSpec((tn, tm), lambda i, j: (clamp(j, vj0), i)))
        out_shape.append(jax.ShapeDtypeStruct((kv_cols, t), BF16))
    return pl.pallas_call(
        functools.partial(_inproj_kernel, rowwise=rowwise, kj0=kj0, vj0=vj0, nkv=nkv, want_vt=want_vt),
        grid=(t // tm, n_tiles),
        in_specs=[pl.BlockSpec((tm, d), lambda i, j: (i, 0), pipeline_mode=pl.Buffered(x_buffers)),
                  mspec, mspec,
                  pl.BlockSpec((1, d), lambda i, j: (0, 0)),
                  pl.BlockSpec((pl.Element(tn), pl.Element(d)),
                               lambda i, j: (pl.multiple_of(j * tn + jnp.where(j >= n_main, gate_cols, 0), 8), 0)),
                  pl.BlockSpec((gate_cols, d), lambda i, j: (main_cols // gate_cols, 0))],
        out_specs=out_specs,
        out_shape=out_shape,
        scratch_shapes=[pltpu.VMEM((tm, d), BF16)],
        compiler_params=_cparams(("parallel", "arbitrary")),
        name="inproj",
    )(x, scale, shift, norm_pre, wt, wt)


def _lambda_value(lamv_ref, lam_init):
    lv = lamv_ref[...]
    d1 = jnp.sum(lv[0:1] * lv[1:2], axis=-1, keepdims=True)
    d2 = jnp.sum(lv[2:3] * lv[3:4], axis=-1, keepdims=True)
    return jnp.exp(d1) - jnp.exp(d2) + lam_init


def _softmax_step(s, shift, v, m_ref, l_ref, acc_ref):
    m_old = m_ref[...]
    m_new = jnp.maximum(m_old, jnp.max(s, axis=-1, keepdims=True) + shift)
    alpha = jnp.exp(m_old - m_new)
    p = jnp.exp(s - (m_new - shift))
    l_ref[...] = alpha * l_ref[...] + jnp.sum(p, axis=-1, keepdims=True)
    acc_ref[...] = alpha * acc_ref[...] + jnp.dot(p.astype(BF16), v, preferred_element_type=F32)
    m_ref[...] = m_new


def _head_post(att, gain, lam_init, z):
    r = att * lax.rsqrt(jnp.mean(att * att, axis=-1, keepdims=True) + EPS) * gain
    return r * (1.0 - lam_init) * _silu(z)


def _attn_prompt_kernel(qi_ref, kj_ref, q_ref, k_ref, vt_ref, z_ref, bias_ref, slope_ref, lamv_ref,
                        gn_ref, o_ref, qs_ref, m_ref, l_ref, acc_ref, *, blk, hd, hp, lam_init):
    e = 2 * hd
    hg = pl.program_id(1)
    p = pl.program_id(2)
    i = qi_ref[p]
    j = kj_ref[p]
    streams = [(t, c) for t in range(hp) for c in range(2)]

    @pl.when(j == 0)
    def _():
        lane = lax.broadcasted_iota(jnp.int32, (blk, e), 1)
        for t in range(hp):
            q = q_ref[:, t * e:(t + 1) * e].astype(F32) * (hd ** -0.5)
            qs_ref[t, 0] = jnp.where(lane < hd, q, 0.0).astype(BF16)
            qs_ref[t, 1] = jnp.where(lane >= hd, q, 0.0).astype(BF16)
        m_ref[...] = jnp.full(m_ref.shape, NEG, F32)
        l_ref[...] = jnp.zeros(l_ref.shape, F32)
        acc_ref[...] = jnp.zeros(acc_ref.shape, F32)

    diag = (j == i).astype(jnp.int32)
    rel = ((j - i) * blk).astype(F32)
    shifts = [slope_ref[hg * hp + t] * rel for t in range(hp)]
    ks = [k_ref[:, t * e:(t + 1) * e] for t in range(hp)]
    logits = [lax.dot_general(ks[t], qs_ref[t, c], _NT, preferred_element_type=F32) + bias_ref[t, diag]
              for (t, c) in streams]
    probs = []
    for (t, c), s in zip(streams, logits):
        m_old = m_ref[t, c]
        m_new = jnp.maximum(m_old, jnp.max(s, axis=0, keepdims=True) + shifts[t])
        alpha = jnp.exp(m_old - m_new)
        pr = jnp.exp(s - (m_new - shifts[t]))
        l_ref[t, c] = alpha * l_ref[t, c] + jnp.sum(pr, axis=0, keepdims=True)
        m_ref[t, c] = m_new
        probs.append((alpha, pr.astype(BF16)))
    for (t, c), (alpha, pr) in zip(streams, probs):
        acc_ref[t, c] = alpha * acc_ref[t, c] + jnp.dot(vt_ref[t * e:(t + 1) * e, :], pr,
                                                        preferred_element_type=F32)

    @pl.when(j == i)
    def _():
        lam = _lambda_value(lamv_ref, lam_init)
        for t in range(hp):
            att = acc_ref[t, 0] / l_ref[t, 0] - lam * (acc_ref[t, 1] / l_ref[t, 1])
            r = att * lax.rsqrt(jnp.mean(att * att, axis=0, keepdims=True) + EPS) * gn_ref[t]
            z = z_ref[:, t * e:(t + 1) * e].astype(F32)
            o_ref[:, t * e:(t + 1) * e] = (r.T * (1.0 - lam_init) * _silu(z)).astype(o_ref.dtype)


def _attn_prompt_call(u, vt, lamv, gn, *, batch, seq, heads, hd, lam_init, cols):
    blk = ATT_BLOCK
    nb = seq // blk
    qi = np.array([i for i in range(nb) for _ in range(i + 1)], np.int32)
    kj = np.array([j for i in range(nb) for j in range(i + 1)], np.int32)
    hp = ATT_HEADS_PER_STEP
    assert heads % hp == 0, (heads, hp)
    slopes = 2.0 ** (-8.0 * np.arange(1, heads + 1, dtype=np.float32) / heads)
    rel = (np.arange(blk)[:, None] - np.arange(blk)[None, :]).astype(np.float32)
    off = slopes[:, None, None] * rel[None]
    dia = np.where(rel[None] <= 0, off, NEG).astype(np.float32)
    bias = jnp.asarray(np.stack([off, dia], axis=1))
    e = 2 * hd
    w = hp * e
    qb, kb, zb = cols["q_a"] // w, cols["k_a"] // w, cols["z_a"] // w

    grid_spec = pltpu.PrefetchScalarGridSpec(
        num_scalar_prefetch=2,
        grid=(batch, heads // hp, len(qi)),
        in_specs=[
            pl.BlockSpec((blk, w), lambda b, h, p, qi, kj: (b * nb + qi[p], qb + h)),
            pl.BlockSpec((blk, w), lambda b, h, p, qi, kj: (b * nb + kj[p], kb + h)),
            pl.BlockSpec((w, blk), lambda b, h, p, qi, kj: (h, b * nb + kj[p])),
            pl.BlockSpec((blk, w), lambda b, h, p, qi, kj: (b * nb + qi[p], zb + h)),
            pl.BlockSpec((hp, 2, blk, blk), lambda b, h, p, qi, kj: (h, 0, 0, 0)),
            pl.BlockSpec(memory_space=pltpu.SMEM),
            pl.BlockSpec(lamv.shape, lambda b, h, p, qi, kj: (0, 0)),
            pl.BlockSpec((hp, e, 1), lambda b, h, p, qi, kj: (h, 0, 0)),
        ],
        out_specs=pl.BlockSpec((blk, w), lambda b, h, p, qi, kj: (b * nb + qi[p], h)),
        scratch_shapes=[pltpu.VMEM((hp, 2, blk, e), BF16), pltpu.VMEM((hp, 2, 1, blk), F32),
                        pltpu.VMEM((hp, 2, 1, blk), F32), pltpu.VMEM((hp, 2, e, blk), F32)],
    )
    return pl.pallas_call(
        functools.partial(_attn_prompt_kernel, blk=blk, hd=hd, hp=hp, lam_init=lam_init),
        grid_spec=grid_spec,
        out_shape=jax.ShapeDtypeStruct((batch * seq, heads * e), BF16),
        compiler_params=_cparams(("parallel", "parallel", "arbitrary")),
        name="attn_prompt",
    )(jnp.asarray(qi), jnp.asarray(kj), u, u, vt, u, bias, jnp.asarray(slopes), lamv, gn)


def _attn_sample_kernel(pt_ref, q_ref, z_ref, kn_ref, vn_ref, *rest, pps, heads, hd, tq, page, lam_init):
    kp_refs = rest[:pps]
    vp_refs = rest[pps:2 * pps]
    (bp_ref, bn_ref, slope_ref, lamv_ref, gn_ref, o_ref, qb_ref, s_ref, m_ref, l_ref, acc_ref) = rest[2 * pps:]
    j = pl.program_id(1)
    e = 2 * hd
    nr = heads * tq
    pw = page * heads

    @pl.when(j == 0)
    def _():
        q = q_ref[...].astype(F32) * (hd ** -0.5)
        lane = lax.broadcasted_iota(jnp.int32, (tq, e), 1)
        for hh in range(heads):
            qh = q[:, hh * e:(hh + 1) * e]
            qb_ref[hh * tq:(hh + 1) * tq, :] = jnp.where(lane < hd, qh, 0.0)
            qb_ref[nr + hh * tq:nr + (hh + 1) * tq, :] = jnp.where(lane >= hd, qh, 0.0)
        m_ref[...] = jnp.full(m_ref.shape, NEG, F32)
        l_ref[...] = jnp.zeros(l_ref.shape, F32)
        acc_ref[...] = jnp.zeros(acc_ref.shape, F32)

    qb = qb_ref[...].astype(BF16)
    shifts = [slope_ref[...] * ((j * pps + s) * page).astype(F32) for s in range(pps)]
    mx = None
    for s in range(pps):
        kp = kp_refs[s][...].reshape(pw, e).astype(BF16)
        sc = lax.dot_general(qb, kp, _NT, preferred_element_type=F32) + bp_ref[...]
        s_ref[:, s * pw:(s + 1) * pw] = sc
        ms = jnp.max(sc, axis=-1, keepdims=True) + shifts[s]
        mx = ms if mx is None else jnp.maximum(mx, ms)
    m_old = m_ref[...]
    m_new = jnp.maximum(m_old, mx)
    alpha = jnp.exp(m_old - m_new)
    lsum = jnp.zeros_like(m_old)
    pv = jnp.zeros(acc_ref.shape, F32)
    for s in range(pps):
        pr = jnp.exp(s_ref[:, s * pw:(s + 1) * pw] - (m_new - shifts[s]))
        lsum = lsum + jnp.sum(pr, axis=-1, keepdims=True)
        vp = vp_refs[s][...].reshape(pw, e).astype(BF16)
        pv = pv + jnp.dot(pr.astype(BF16), vp, preferred_element_type=F32)
    l_ref[...] = alpha * l_ref[...] + lsum
    acc_ref[...] = alpha * acc_ref[...] + pv
    m_ref[...] = m_new

    @pl.when(j == pl.num_programs(1) - 1)
    def _():
        kn = kn_ref[...]
        vn = vn_ref[...]
        knr = jnp.concatenate([kn[:, hh * e:(hh + 1) * e] for hh in range(heads)], axis=0).astype(BF16)
        vnr = jnp.concatenate([vn[:, hh * e:(hh + 1) * e] for hh in range(heads)], axis=0).astype(BF16)
        sc = lax.dot_general(qb, knr, _NT, preferred_element_type=F32) + bn_ref[...]
        _softmax_step(sc, 0.0, vnr, m_ref, l_ref, acc_ref)
        o = acc_ref[...] / l_ref[...]
        lam = _lambda_value(lamv_ref, lam_init)
        att = o[0:nr] - lam * o[nr:2 * nr]
        z = z_ref[...].astype(F32)
        for hh in range(heads):
            r = _head_post(att[hh * tq:(hh + 1) * tq], gn_ref[hh:hh + 1, :], lam_init,
                           z[:, hh * e:(hh + 1) * e])
            o_ref[:, hh * e:(hh + 1) * e] = r.astype(o_ref.dtype)


def _attn_sample_call(u, k_new, v_new, cache_k, cache_v, page_table, lamv, gn, *, tq, heads, hd,
                      lam_init, cols):
    bd, npg = page_table.shape
    page = cache_k.shape[1]
    pps = PAGES_PER_STEP
    e = 2 * hd
    nr = heads * tq
    past = npg * page
    slopes = 2.0 ** (-8.0 * np.arange(1, heads + 1, dtype=np.float32) / heads)
    r_head = (np.arange(2 * nr) % nr) // tq
    r_tok = np.arange(2 * nr) % tq
    r_slope = slopes[r_head]
    c_tok, c_head = np.arange(page * heads) // heads, np.arange(page * heads) % heads
    bp = r_slope[:, None] * (c_tok[None, :] - (past + r_tok)[:, None])
    bp = np.where(r_head[:, None] == c_head[None, :], bp, NEG).astype(np.float32)
    n_head, n_tok = np.arange(nr) // tq, np.arange(nr) % tq
    bn = r_slope[:, None] * (n_tok[None, :] - r_tok[:, None])
    ok = (r_head[:, None] == n_head[None, :]) & (n_tok[None, :] <= r_tok[:, None])
    bn = np.where(ok, bn, NEG).astype(np.float32)
    slope_col = r_slope.reshape(2 * nr, 1).astype(np.float32)
    w = heads * e

    def page_spec(s):
        return pl.BlockSpec((None, page, heads, e),
                            lambda b, j, pt: (pt[b * npg + j * pps + s], 0, 0, 0))

    const2 = lambda b, j, pt: (0, 0)
    grid_spec = pltpu.PrefetchScalarGridSpec(
        num_scalar_prefetch=1,
        grid=(bd, npg // pps),
        in_specs=[pl.BlockSpec((tq, w), lambda b, j, pt: (b, cols["q_a"] // w)),
                  pl.BlockSpec((tq, w), lambda b, j, pt: (b, cols["z_a"] // w)),
                  pl.BlockSpec((tq, w), lambda b, j, pt: (b, 0)),
                  pl.BlockSpec((tq, w), lambda b, j, pt: (b, 0))]
                 + [page_spec(s) for s in range(pps)] + [page_spec(s) for s in range(pps)]
                 + [pl.BlockSpec(bp.shape, const2), pl.BlockSpec(bn.shape, const2),
                    pl.BlockSpec(slope_col.shape, const2), pl.BlockSpec(lamv.shape, const2),
                    pl.BlockSpec(gn.shape, const2)],
        out_specs=pl.BlockSpec((tq, w), lambda b, j, pt: (b, 0)),
        scratch_shapes=[pltpu.VMEM((2 * nr, e), F32), pltpu.VMEM((2 * nr, pps * page * heads), F32),
                        pltpu.VMEM((2 * nr, 1), F32), pltpu.VMEM((2 * nr, 1), F32),
                        pltpu.VMEM((2 * nr, e), F32)],
    )
    return pl.pallas_call(
        functools.partial(_attn_sample_kernel, pps=pps, heads=heads, hd=hd, tq=tq, page=page,
                          lam_init=lam_init),
        grid_spec=grid_spec,
        out_shape=jax.ShapeDtypeStruct((bd * tq, w), F32),
        compiler_params=_cparams(("parallel", "arbitrary")),
        name="attn_sample",
    )(page_table.reshape(-1), u, u, k_new, v_new, *([cache_k] * pps), *([cache_v] * pps),
      jnp.asarray(bp), jnp.asarray(bn), jnp.asarray(slope_col), lamv, gn)


def _mlstm_chunks(chains, scale):
    ln = chains[0][0].shape[0]
    row = lax.broadcasted_iota(jnp.int32, (ln, ln), 0)
    col = lax.broadcasted_iota(jnp.int32, (ln, ln), 1)
    tri = col <= row
    eye = col == row

    gate = []
    for (q, k, v, ig_row, lf_row, c_st, n_st, m_st) in chains:
        lf_b = jnp.broadcast_to(lf_row, (ln, ln))
        ig_b = jnp.broadcast_to(ig_row, (ln, ln))
        b_col = jnp.sum(jnp.where(tri, lf_b, 0.0), axis=1, keepdims=True)
        lf_col = jnp.sum(jnp.where(eye, lf_b, 0.0), axis=1, keepdims=True)
        ig_col = jnp.sum(jnp.where(eye, ig_b, 0.0), axis=1, keepdims=True)
        b_row = jnp.sum(jnp.where(row <= col, jnp.broadcast_to(lf_col, (ln, ln)), 0.0),
                        axis=0, keepdims=True)
        dmat = jnp.where(tri, b_col - b_row + ig_row, NEG)
        a_col = b_col + m_st
        m_t = jnp.maximum(a_col, jnp.max(dmat, axis=1, keepdims=True))
        m_new = m_t[ln - 1:ln, :]
        b_last = b_col[ln - 1:ln, :]
        wk = jnp.exp(b_last - b_col + ig_col - m_new)
        decay = jnp.exp(b_last + m_st - m_new)
        gate.append((jnp.exp(dmat - m_t), jnp.exp(a_col - m_t), m_t, m_new, wk, decay))

    first = []
    for (q, k, v, _, _, c_st, _, _), (_, _, _, _, wk, _) in zip(chains, gate):
        qk = lax.dot_general(q, k, _NT, preferred_element_type=F32)
        q_c = lax.dot_general(q, c_st.astype(BF16), _NT, preferred_element_type=F32)
        vw = (v.astype(F32) * wk).astype(BF16)
        vk = lax.dot_general(vw, k, _TN, preferred_element_type=F32)
        first.append((qk, q_c, vk))

    out = []
    for (q, k, v, _, _, c_st, n_st, _), (dexp, inter, m_t, m_new, wk, decay), (qk, q_c, vk) in zip(
            chains, gate, first):
        s = qk * scale * dexp
        num = inter * q_c + jnp.dot(s.astype(BF16), v, preferred_element_type=F32)
        den = (inter * jnp.sum(q.astype(F32) * n_st, axis=1, keepdims=True)
               + jnp.sum(s, axis=1, keepdims=True))
        h = num / jnp.maximum(jnp.abs(den), jnp.exp(-m_t))
        c_new = decay * c_st + vk * scale
        n_new = decay * n_st + jnp.sum(k.astype(F32) * wk, axis=0, keepdims=True) * scale
        out.append((h, c_new, n_new, m_new))
    return out


def _log_sigmoid(x):
    return jnp.minimum(x, 0.0) - jnp.log(1.0 + jnp.exp(-jnp.abs(x)))


def _mlstm_post(h, gain, o, z):
    hn = h * lax.rsqrt(jnp.mean(h * h, axis=-1, keepdims=True) + EPS) * gain
    return _sigmoid(o) * hn * _silu(z)


def _mlstm_prompt_kernel(gb_ref, q_ref, k_ref, v_ref, o_ref, z_ref, g_ref, gn_ref, c0_ref, n0_ref, m0_ref,
                         y_ref, c_out, n_out, m_out, c_s, n_s, m_s, *, scale, heads, dk, dv):
    ci = pl.program_id(1)

    @pl.when(ci == 0)
    def _():
        c_s[...] = c0_ref[...]
        n_s[...] = n0_ref[...]
        m_s[...] = m0_ref[...]

    g = g_ref[...]
    chains = []
    for hh in range(heads):
        ks = slice(hh * dk, (hh + 1) * dk)
        vs = slice(hh * dv, (hh + 1) * dv)
        ig = g[hh:hh + 1, :] + gb_ref[0, hh]
        lf = _log_sigmoid(g[heads + hh:heads + hh + 1, :] + gb_ref[1, hh])
        chains.append((q_ref[:, ks], k_ref[:, ks], v_ref[:, vs], ig, lf, c_s[hh], n_s[hh], m_s[hh]))
    for hh, (h, c_new, n_new, m_new) in enumerate(_mlstm_chunks(chains, scale)):
        vs = slice(hh * dv, (hh + 1) * dv)
        c_s[hh] = c_new
        n_s[hh] = n_new
        m_s[hh] = m_new
        y_ref[:, vs] = _mlstm_post(h, gn_ref[hh], o_ref[:, vs].astype(F32),
                                   z_ref[:, vs].astype(F32)).astype(y_ref.dtype)

    @pl.when(ci == pl.num_programs(1) - 1)
    def _():
        c_out[...] = c_s[...]
        n_out[...] = n_s[...]
        m_out[...] = m_s[...]


def _mlstm_prompt_call(u, gates, gb, gn, c0, n0, m0, *, batch, seq, heads, dk, dv, cols):
    ln = MLSTM_CHUNK
    nc = seq // ln
    wk, wv = heads * dk, heads * dv
    qb, kb, vb, ob, zb = (cols["q_m"] // wk, cols["k_m"] // wk, cols["v_m"] // wv, cols["o_m"] // wv,
                          cols["z_m"] // wv)
    st4 = lambda b, c: (b, 0, 0, 0)
    return pl.pallas_call(
        functools.partial(_mlstm_prompt_kernel, scale=dk ** -0.5, heads=heads, dk=dk, dv=dv),
        grid=(batch, nc),
        in_specs=[pl.BlockSpec(memory_space=pltpu.SMEM),
                  pl.BlockSpec((ln, wk), lambda b, c: (b * nc + c, qb)),
                  pl.BlockSpec((ln, wk), lambda b, c: (b * nc + c, kb)),
                  pl.BlockSpec((ln, wv), lambda b, c: (b * nc + c, vb)),
                  pl.BlockSpec((ln, wv), lambda b, c: (b * nc + c, ob)),
                  pl.BlockSpec((ln, wv), lambda b, c: (b * nc + c, zb)),
                  pl.BlockSpec((None, None, 2 * heads, ln), lambda b, c: (b, c, 0, 0)),
                  pl.BlockSpec((heads, 1, dv), lambda b, c: (0, 0, 0)),
                  pl.BlockSpec((None, heads, dv, dk), st4),
                  pl.BlockSpec((None, heads, 1, dk), st4),
                  pl.BlockSpec((None, heads, 1, 1), st4)],
        out_specs=[pl.BlockSpec((ln, wv), lambda b, c: (b * nc + c, 0)),
                   pl.BlockSpec((None, heads, dv, dk), st4),
                   pl.BlockSpec((None, heads, 1, dk), st4),
                   pl.BlockSpec((None, heads, 1, 1), st4)],
        out_shape=[jax.ShapeDtypeStruct((batch * seq, wv), BF16),
                   jax.ShapeDtypeStruct((batch, heads, dv, dk), F32),
                   jax.ShapeDtypeStruct((batch, heads, 1, dk), F32),
                   jax.ShapeDtypeStruct((batch, heads, 1, 1), F32)],
        scratch_shapes=[pltpu.VMEM((heads, dv, dk), F32), pltpu.VMEM((heads, 1, dk), F32),
                        pltpu.VMEM((heads, 1, 1), F32)],
        compiler_params=_cparams(("parallel", "arbitrary")),
        name="mlstm_prompt",
    )(gb, u, u, u, u, u, gates, gn, c0, n0, m0)


def _mlstm_sample_kernel(gb_ref, q_ref, k_ref, v_ref, o_ref, z_ref, g_ref, gn_ref, c0_ref, n0_ref, m0_ref,
                         y_ref, c_out, n_out, m_out, *, scale, bb, tq):
    hh = pl.program_id(1)
    q = q_ref[...].astype(F32)
    k = k_ref[...].astype(F32)
    v = v_ref[...].astype(F32)
    o = o_ref[...].astype(F32)
    z = z_ref[...].astype(F32)
    gain = gn_ref[...]
    chains = []
    for i in range(bb):
        rs = slice(i * tq, (i + 1) * tq)
        g = g_ref[i]
        ig = g[0:1, :] + gb_ref[0, hh]
        lf = _log_sigmoid(g[1:2, :] + gb_ref[1, hh])
        chains.append((q[rs].astype(BF16), k[rs].astype(BF16), v[rs].astype(BF16), ig, lf,
                       c0_ref[i], n0_ref[i], m0_ref[i]))
    for i, (h, c_new, n_new, m_new) in enumerate(_mlstm_chunks(chains, scale)):
        rs = slice(i * tq, (i + 1) * tq)
        y_ref[rs, :] = _mlstm_post(h, gain, o[rs], z[rs]).astype(y_ref.dtype)
        c_out[i] = c_new
        n_out[i] = n_new
        m_out[i] = m_new


def _mlstm_sample_call(u, gates, gb, gn, c0, n0, m0, *, tq, heads, dk, dv, cols):
    bd = c0.shape[0]
    bb = MLSTM_DEC_BB
    assert bd % bb == 0, (bd, bb)
    qb, kb, vb, ob, zb = (cols["q_m"] // dk, cols["k_m"] // dk, cols["v_m"] // dv, cols["o_m"] // dv,
                          cols["z_m"] // dv)
    st4 = lambda g, h: (g, h, 0, 0)
    return pl.pallas_call(
        functools.partial(_mlstm_sample_kernel, scale=dk ** -0.5, bb=bb, tq=tq),
        grid=(bd // bb, heads),
        in_specs=[pl.BlockSpec(memory_space=pltpu.SMEM),
                  pl.BlockSpec((bb * tq, dk), lambda g, h: (g, qb + h)),
                  pl.BlockSpec((bb * tq, dk), lambda g, h: (g, kb + h)),
                  pl.BlockSpec((bb * tq, dv), lambda g, h: (g, vb + h)),
                  pl.BlockSpec((bb * tq, dv), lambda g, h: (g, ob + h)),
                  pl.BlockSpec((bb * tq, dv), lambda g, h: (g, zb + h)),
                  pl.BlockSpec((bb, None, 2, tq), st4),
                  pl.BlockSpec((None, 1, dv), lambda g, h: (h, 0, 0)),
                  pl.BlockSpec((bb, None, dv, dk), st4),
                  pl.BlockSpec((bb, None, 1, dk), st4),
                  pl.BlockSpec((bb, None, 1, 1), st4)],
        out_specs=[pl.BlockSpec((bb * tq, dv), lambda g, h: (g, h)),
                   pl.BlockSpec((bb, None, dv, dk), st4),
                   pl.BlockSpec((bb, None, 1, dk), st4),
                   pl.BlockSpec((bb, None, 1, 1), st4)],
        out_shape=[jax.ShapeDtypeStruct((bd * tq, heads * dv), F32),
                   jax.ShapeDtypeStruct((bd, heads, dv, dk), F32),
                   jax.ShapeDtypeStruct((bd, heads, 1, dk), F32),
                   jax.ShapeDtypeStruct((bd, heads, 1, 1), F32)],
        compiler_params=_cparams(("parallel", "parallel")),
        name="mlstm_sample",
    )(gb, u, u, u, u, u, gates, gn, c0, n0, m0)


def _out_kernel(x_ref, ya_ref, ym_ref, ga_ref, gm_ref, gate_ref, np_ref, wa_ref, wm_ref, wo_ref, o_ref,
                *, rowwise):
    ya = jnp.dot(ya_ref[...].astype(BF16), wa_ref[...], preferred_element_type=F32)
    ym = jnp.dot(ym_ref[...].astype(BF16), wm_ref[...], preferred_element_type=F32)
    y2 = _sigmoid(ga_ref[...].astype(F32)) * ya + _sigmoid(gm_ref[...].astype(F32)) * ym
    y = jnp.dot(y2.astype(BF16), wo_ref[...], preferred_element_type=F32)
    yn = y * lax.rsqrt(jnp.mean(y * y, axis=-1, keepdims=True) + EPS) * np_ref[...]
    if rowwise:
        g, d = gate_ref.shape
        gy = (gate_ref[...][:, None, :] * yn.reshape(g, yn.shape[0] // g, d)).reshape(yn.shape)
    else:
        gy = gate_ref[0] * yn
    o_ref[...] = x_ref[...] + gy


def _out_call(x, ya, ym, u, gate, norm_post, wa, wm, wo, *, rowwise, rows_per_group, tm, cols):
    t, d = x.shape
    da = ya.shape[1]
    ga_blk, gm_blk = cols["g_a"] // d, cols["g_m"] // d
    if rowwise:
        gspec = pl.BlockSpec((tm // rows_per_group, d), lambda i: (i, 0))
    else:
        gspec = pl.BlockSpec((1, 1, d), lambda i: ((i * tm) // rows_per_group, 0, 0))
    resident = dict(pipeline_mode=pl.Buffered(1))
    return pl.pallas_call(
        functools.partial(_out_kernel, rowwise=rowwise),
        grid=(t // tm,),
        in_specs=[pl.BlockSpec((tm, d), lambda i: (i, 0)),
                  pl.BlockSpec((tm, da), lambda i: (i, 0)),
                  pl.BlockSpec((tm, da), lambda i: (i, 0)),
                  pl.BlockSpec((tm, d), lambda i: (i, ga_blk)),
                  pl.BlockSpec((tm, d), lambda i: (i, gm_blk)),
                  gspec,
                  pl.BlockSpec((1, d), lambda i: (0, 0)),
                  pl.BlockSpec(wa.shape, lambda i: (0, 0), **resident),
                  pl.BlockSpec(wm.shape, lambda i: (0, 0), **resident),
                  pl.BlockSpec(wo.shape, lambda i: (0, 0), **resident)],
        out_specs=pl.BlockSpec((tm, d), lambda i: (i, 0)),
        out_shape=jax.ShapeDtypeStruct((t, d), F32),
        compiler_params=_cparams(("parallel",)),
        name="out_stage",
    )(x, ya, ym, u, u, gate, norm_post, wa, wm, wo)


def kernel(x_prompt, x_sample, cache_k, cache_v, state_C, state_n, state_m, page_table, c_prompt, c_sample,
           w_ada, b_ada, norm_pre, norm_post, w_in, b_igate, b_fgate, lambda_q1, lambda_k1, lambda_q2,
           lambda_k2, attn_head_norm, mlstm_head_norm, w_br_a, w_br_m, w_out):
    depth = w_in.shape[0]
    assert depth == 1, "single-layer trunk"
    batch, seq, d = x_prompt.shape
    bd, tq, _ = x_sample.shape
    heads_a, e = cache_k.shape[3], cache_k.shape[4]
    hd = e // 2
    d_att = heads_a * e
    heads_m, dv, dk = state_C.shape[2], state_C.shape[3], state_C.shape[4]
    d_mv, d_mqk = heads_m * dv, heads_m * dk
    layer = 0
    lam_init = 0.8 - 0.6 * math.exp(-0.3 * layer)

    names = ("q_a", "k_a", "v_a", "z_a", "q_m", "k_m", "v_m", "o_m", "z_m", "i_m", "f_m", "g_a", "g_m")
    widths = (d_att, d_att, d_att, d_att, d_mqk, d_mqk, d_mv, d_mv, d_mv, heads_m, heads_m, d, d)
    src = dict(zip(names, np.concatenate([[0], np.cumsum(widths)[:-1]]).tolist()))
    main_cols = src["i_m"]
    cols = {nm: src[nm] for nm in names[:9]}
    cols["g_a"], cols["g_m"] = main_cols, main_cols + d

    wt = w_in.reshape(w_in.shape[1:]).T
    inproj = functools.partial(_inproj_call, main_cols=main_cols, gate_cols=2 * heads_m, k_col=src["k_a"],
                               v_col=src["v_a"], kv_cols=d_att, tn=512)
    wa = w_br_a[layer].astype(BF16)
    wm = w_br_m[layer].astype(BF16)
    wo = w_out[layer].astype(BF16)

    n_c = batch + bd
    pad = (-n_c) % 8
    c_all = jnp.concatenate([c_prompt, c_sample, jnp.zeros((pad, d), F32)], axis=0)
    mod = _mod_call(c_all, w_ada[layer], b_ada[layer].reshape(1, -1))
    shift, scale, gate = mod[:, :d], mod[:, d:2 * d], mod[:, 2 * d:]

    lamv = jnp.stack([lambda_q1[layer], lambda_k1[layer], lambda_q2[layer], lambda_k2[layer]])
    gn_a = attn_head_norm[layer].reshape(heads_a, e)
    gn_m = mlstm_head_norm[layer].reshape(heads_m, 1, dv)
    gb = jnp.stack([b_igate[layer], b_fgate[layer]])
    npre = norm_pre[layer].reshape(1, d)
    npost = norm_post[layer].reshape(1, d)

    xp = x_prompt.reshape(batch * seq, d)
    grp = lambda a: a[:batch].reshape(batch, 1, d)
    u_p, k_p, v_p, g_p, vt_p = inproj(xp, grp(scale), grp(shift), npre, wt, rowwise=False,
                                      rows_per_group=seq, tm=1024, u_dtype=BF16, want_vt=True, x_buffers=1)
    ya_p = _attn_prompt_call(u_p, vt_p, lamv, gn_a.reshape(heads_a, e, 1), batch=batch, seq=seq,
                             heads=heads_a, hd=hd, lam_init=lam_init, cols=cols)
    nc = seq // MLSTM_CHUNK
    gates_p = g_p.reshape(batch, nc, MLSTM_CHUNK, 2 * heads_m).transpose(0, 1, 3, 2)
    ym_p, c_p, n_p, m_p = _mlstm_prompt_call(
        u_p, gates_p, gb, gn_m, jnp.zeros((batch, heads_m, dv, dk), F32), jnp.zeros((batch, heads_m, 1, dk), F32),
        jnp.zeros((batch, heads_m, 1, 1), F32), batch=batch, seq=seq, heads=heads_m, dk=dk, dv=dv, cols=cols)
    y_p = _out_call(xp, ya_p, ym_p, u_p, grp(gate), npost, wa, wm, wo, rowwise=False, rows_per_group=seq,
                    tm=512, cols=cols)

    xs = x_sample.reshape(bd * tq, d)
    rep = lambda a: a[batch:batch + bd]
    u_s, k_s, v_s, g_s = inproj(xs, rep(scale), rep(shift), npre, wt, rowwise=True,
                                rows_per_group=tq, tm=min(1024, bd * tq), u_dtype=F32, want_vt=False,
                                x_buffers=1)
    ya_s = _attn_sample_call(u_s, k_s, v_s, cache_k.reshape(cache_k.shape[1:]),
                             cache_v.reshape(cache_v.shape[1:]), page_table, lamv, gn_a,
                             tq=tq, heads=heads_a, hd=hd, lam_init=lam_init, cols=cols)
    gates_s = g_s.reshape(bd, tq, 2, heads_m).transpose(0, 3, 2, 1)
    ym_s, c_s, n_s, m_s = _mlstm_sample_call(
        u_s, gates_s, gb, gn_m, state_C.reshape(state_C.shape[1:]), state_n.reshape(bd, heads_m, 1, dk),
        state_m.reshape(bd, heads_m, 1, 1), tq=tq, heads=heads_m, dk=dk, dv=dv, cols=cols)
    y_s = _out_call(xs, ya_s, ym_s, u_s, rep(gate), npost, wa, wm, wo, rowwise=True, rows_per_group=tq,
                    tm=min(256, bd * tq), cols=cols)

    return (y_p.reshape(batch, seq, d), y_s.reshape(bd, tq, d),
            k_p.reshape(1, batch, seq, heads_a, e), v_p.reshape(1, batch, seq, heads_a, e),
            c_p.reshape(1, batch, heads_m, dv, dk), n_p.reshape(1, batch, heads_m, dk),
            m_p.reshape(1, batch, heads_m),
            k_s.reshape(1, bd, tq, heads_a, e), v_s.reshape(1, bd, tq, heads_a, e),
            c_s.reshape(1, bd, heads_m, dv, dk), n_s.reshape(1, bd, heads_m, dk), m_s.reshape(1, bd, heads_m))
```

```python
import functools
import math

import jax
import jax.numpy as jnp
import numpy as np
from jax import lax
from jax.experimental import pallas as pl
from jax.experimental.pallas import tpu as pltpu

F32 = jnp.float32
BF16 = jnp.bfloat16
EPS = 1e-6
NEG = -1e30
VMEM_LIMIT = 56 * 1024 * 1024
LANE = 128
MLSTM_CHUNK = 256
ATT_BLOCK = 512
ATT_HEADS_PER_STEP = 4
PAGES_PER_STEP = 16
MLSTM_DEC_BB = 16

_NT = (((1,), (1,)), ((), ()))
_TN = (((0,), (0,)), ((), ()))


def _cparams(sem):
    return pltpu.CompilerParams(dimension_semantics=sem, vmem_limit_bytes=VMEM_LIMIT)


def _sigmoid(x):
    return 1.0 / (1.0 + jnp.exp(-x))


def _silu(x):
    return x * _sigmoid(x)


def _mod_kernel(c_ref, w_ref, b_ref, o_ref):
    a = _silu(c_ref[...]).astype(BF16)
    o_ref[...] = jnp.dot(a, w_ref[...].astype(BF16), preferred_element_type=F32) + b_ref[...]


def _mod_call(c_all, w_ada, b_ada):
    rows, d = c_all.shape
    n = w_ada.shape[1]
    tn = 512
    return pl.pallas_call(
        _mod_kernel,
        grid=(n // tn,),
        in_specs=[pl.BlockSpec((rows, d), lambda j: (0, 0)),
                  pl.BlockSpec((d, tn), lambda j: (0, j)),
                  pl.BlockSpec((1, tn), lambda j: (0, j))],
        out_specs=pl.BlockSpec((rows, tn), lambda j: (0, j)),
        out_shape=jax.ShapeDtypeStruct((rows, n), F32),
        compiler_params=_cparams(("parallel",)),
        name="mod",
    )(c_all, w_ada, b_ada)


def _inproj_kernel(x_ref, sc_ref, sh_ref, g_ref, w_ref, wg_ref, u_ref, k_ref, v_ref, gate_ref, *rest,
                   rowwise, kj0, vj0, nkv, want_vt):
    if want_vt:
        vt_ref, h_ref = rest
    else:
        (h_ref,) = rest
    j = pl.program_id(1)

    @pl.when(j == 0)
    def _():
        x = x_ref[...]
        xn = x * lax.rsqrt(jnp.mean(x * x, axis=-1, keepdims=True) + EPS) * g_ref[...]
        if rowwise:
            g, d = sc_ref.shape
            xg = xn.reshape(g, xn.shape[0] // g, d)
            h = (xg * (1.0 + sc_ref[...][:, None, :]) + sh_ref[...][:, None, :]).reshape(xn.shape).astype(BF16)
        else:
            h = (xn * (1.0 + sc_ref[0]) + sh_ref[0]).astype(BF16)
        h_ref[...] = h
        gate_ref[...] = lax.dot_general(h, wg_ref[...].astype(BF16), _NT, preferred_element_type=F32)

    res = lax.dot_general(h_ref[...], w_ref[...].astype(BF16), _NT, preferred_element_type=F32)
    u_ref[...] = res.astype(u_ref.dtype)

    @pl.when((j >= kj0) & (j < kj0 + nkv))
    def _():
        k_ref[...] = res

    @pl.when((j >= vj0) & (j < vj0 + nkv))
    def _():
        v_ref[...] = res
        if want_vt:
            vt_ref[...] = res.T.astype(BF16)


def _inproj_call(x, scale, shift, norm_pre, wt, *, rowwise, rows_per_group, tm, tn, u_dtype,
                 main_cols, gate_cols, k_col, v_col, kv_cols, want_vt, x_buffers):
    t, d = x.shape
    n_main = main_cols // tn
    n_tiles = (wt.shape[0] - gate_cols) // tn
    kj0, vj0, nkv = k_col // tn, v_col // tn, kv_cols // tn
    clamp = lambda j, j0: jnp.clip(j - j0, 0, nkv - 1)
    if rowwise:
        mspec = pl.BlockSpec((tm // rows_per_group, d), lambda i, j: (i, 0))
    else:
        mspec = pl.BlockSpec((1, 1, d), lambda i, j: ((i * tm) // rows_per_group, 0, 0))
    out_specs = [pl.BlockSpec((tm, tn), lambda i, j: (i, j)),
                 pl.BlockSpec((tm, tn), lambda i, j: (i, clamp(j, kj0))),
                 pl.BlockSpec((tm, tn), lambda i, j: (i, clamp(j, vj0))),
                 pl.BlockSpec((tm, gate_cols), lambda i, j: (i, 0))]
    out_shape = [jax.ShapeDtypeStruct((t, n_tiles * tn), u_dtype),
                 jax.ShapeDtypeStruct((t, kv_cols), F32), jax.ShapeDtypeStruct((t, kv_cols), F32),
                 jax.ShapeDtypeStruct((t, gate_cols), F32)]
    if want_vt:
        out_specs.append(pl.BlockSpec((tn, tm), lambda i, j: (clamp(j, vj0), i)))
        out_shape.append(jax.ShapeDtypeStruct((kv_cols, t), BF16))
    return pl.pallas_call(
        functools.partial(_inproj_kernel, rowwise=rowwise, kj0=kj0, vj0=vj0, nkv=nkv, want_vt=want_vt),
        grid=(t // tm, n_tiles),
        in_specs=[pl.BlockSpec((tm, d), lambda i, j: (i, 0), pipeline_mode=pl.Buffered(x_buffers)),
                  mspec, mspec,
                  pl.BlockSpec((1, d), lambda i, j: (0, 0)),
                  pl.BlockSpec((pl.Element(tn), pl.Element(d)),
                               lambda i, j: (pl.multiple_of(j * tn + jnp.where(j >= n_main, gate_cols, 0), 8), 0)),
                  pl.BlockSpec((gate_cols, d), lambda i, j: (main_cols // gate_cols, 0))],
        out_specs=out_specs,
        out_shape=out_shape,
        scratch_shapes=[pltpu.VMEM((tm, d), BF16)],
        compiler_params=_cparams(("parallel", "arbitrary")),
        name="inproj",
    )(x, scale, shift, norm_pre, wt, wt)


def _lambda_value(lamv_ref, lam_init):
    lv = lamv_ref[...]
    d1 = jnp.sum(lv[0:1] * lv[1:2], axis=-1, keepdims=True)
    d2 = jnp.sum(lv[2:3] * lv[3:4], axis=-1, keepdims=True)
    return jnp.exp(d1) - jnp.exp(d2) + lam_init


def _softmax_step(s, shift, v, m_ref, l_ref, acc_ref):
    m_old = m_ref[...]
    m_new = jnp.maximum(m_old, jnp.max(s, axis=-1, keepdims=True) + shift)
    alpha = jnp.exp(m_old - m_new)
    p = jnp.exp(s - (m_new - shift))
    l_ref[...] = alpha * l_ref[...] + jnp.sum(p, axis=-1, keepdims=True)
    acc_ref[...] = alpha * acc_ref[...] + jnp.dot(p.astype(BF16), v, preferred_element_type=F32)
    m_ref[...] = m_new


def _head_post(att, gain, lam_init, z):
    r = att * lax.rsqrt(jnp.mean(att * att, axis=-1, keepdims=True) + EPS) * gain
    return r * (1.0 - lam_init) * _silu(z)


def _attn_prompt_kernel(qi_ref, kj_ref, q_ref, k_ref, vt_ref, z_ref, bias_ref, slope_ref, lamv_ref,
                        gn_ref, o_ref, qs_ref, m_ref, l_ref, acc_ref, *, blk, hd, hp, lam_init):
    e = 2 * hd
    hg = pl.program_id(1)
    p = pl.program_id(2)
    i = qi_ref[p]
    j = kj_ref[p]
    streams = [(t, c) for t in range(hp) for c in range(2)]

    @pl.when(j == 0)
    def _():
        lane = lax.broadcasted_iota(jnp.int32, (blk, e), 1)
        for t in range(hp):
            q = q_ref[:, t * e:(t + 1) * e].astype(F32) * (hd ** -0.5)
            qs_ref[t, 0] = jnp.where(lane < hd, q, 0.0).astype(BF16)
            qs_ref[t, 1] = jnp.where(lane >= hd, q, 0.0).astype(BF16)
        m_ref[...] = jnp.full(m_ref.shape, NEG, F32)
        l_ref[...] = jnp.zeros(l_ref.shape, F32)
        acc_ref[...] = jnp.zeros(acc_ref.shape, F32)

    diag = (j == i).astype(jnp.int32)
    rel = ((j - i) * blk).astype(F32)
    shifts = [slope_ref[hg * hp + t] * rel for t in range(hp)]
    ks = [k_ref[:, t * e:(t + 1) * e] for t in range(hp)]
    logits = [lax.dot_general(ks[t], qs_ref[t, c], _NT, preferred_element_type=F32) + bias_ref[t, diag]
              for (t, c) in streams]
    probs = []
    for (t, c), s in zip(streams, logits):
        m_old = m_ref[t, c]
        m_new = jnp.maximum(m_old, jnp.max(s, axis=0, keepdims=True) + shifts[t])
        alpha = jnp.exp(m_old - m_new)
        pr = jnp.exp(s - (m_new - shifts[t]))
        l_ref[t, c] = alpha * l_ref[t, c] + jnp.sum(pr, axis=0, keepdims=True)
        m_ref[t, c] = m_new
        probs.append((alpha, pr.astype(BF16)))
    for (t, c), (alpha, pr) in zip(streams, probs):
        acc_ref[t, c] = alpha * acc_ref[t, c] + jnp.dot(vt_ref[t * e:(t + 1) * e, :], pr,
                                                        preferred_element_type=F32)

    @pl.when(j == i)
    def _():
        lam = _lambda_value(lamv_ref, lam_init)
        for t in range(hp):
            att = acc_ref[t, 0] / l_ref[t, 0] - lam * (acc_ref[t, 1] / l_ref[t, 1])
            r = att * lax.rsqrt(jnp.mean(att * att, axis=0, keepdims=True) + EPS) * gn_ref[t]
            z = z_ref[:, t * e:(t + 1) * e].astype(F32)
            o_ref[:, t * e:(t + 1) * e] = (r.T * (1.0 - lam_init) * _silu(z)).astype(o_ref.dtype)


def _attn_prompt_call(u, vt, lamv, gn, *, batch, seq, heads, hd, lam_init, cols):
    blk = ATT_BLOCK
    nb = seq // blk
    qi = np.array([i for i in range(nb) for _ in range(i + 1)], np.int32)
    kj = np.array([j for i in range(nb) for j in range(i + 1)], np.int32)
    hp = ATT_HEADS_PER_STEP
    assert heads % hp == 0, (heads, hp)
    slopes = 2.0 ** (-8.0 * np.arange(1, heads + 1, dtype=np.float32) / heads)
    rel = (np.arange(blk)[:, None] - np.arange(blk)[None, :]).astype(np.float32)
    off = slopes[:, None, None] * rel[None]
    dia = np.where(rel[None] <= 0, off, NEG).astype(np.float32)
    bias = jnp.asarray(np.stack([off, dia], axis=1))
    e = 2 * hd
    w = hp * e
    qb, kb, zb = cols["q_a"] // w, cols["k_a"] // w, cols["z_a"] // w

    grid_spec = pltpu.PrefetchScalarGridSpec(
        num_scalar_prefetch=2,
        grid=(batch, heads // hp, len(qi)),
        in_specs=[
            pl.BlockSpec((blk, w), lambda b, h, p, qi, kj: (b * nb + qi[p], qb + h)),
            pl.BlockSpec((blk, w), lambda b, h, p, qi, kj: (b * nb + kj[p], kb + h)),
            pl.BlockSpec((w, blk), lambda b, h, p, qi, kj: (h, b * nb + kj[p])),
            pl.BlockSpec((blk, w), lambda b, h, p, qi, kj: (b * nb + qi[p], zb + h)),
            pl.BlockSpec((hp, 2, blk, blk), lambda b, h, p, qi, kj: (h, 0, 0, 0)),
            pl.BlockSpec(memory_space=pltpu.SMEM),
            pl.BlockSpec(lamv.shape, lambda b, h, p, qi, kj: (0, 0)),
            pl.BlockSpec((hp, e, 1), lambda b, h, p, qi, kj: (h, 0, 0)),
        ],
        out_specs=pl.BlockSpec((blk, w), lambda b, h, p, qi, kj: (b * nb + qi[p], h)),
        scratch_shapes=[pltpu.VMEM((hp, 2, blk, e), BF16), pltpu.VMEM((hp, 2, 1, blk), F32),
                        pltpu.VMEM((hp, 2, 1, blk), F32), pltpu.VMEM((hp, 2, e, blk), F32)],
    )
    return pl.pallas_call(
        functools.partial(_attn_prompt_kernel, blk=blk, hd=hd, hp=hp, lam_init=lam_init),
        grid_spec=grid_spec,
        out_shape=jax.ShapeDtypeStruct((batch * seq, heads * e), BF16),
        compiler_params=_cparams(("parallel", "parallel", "arbitrary")),
        name="attn_prompt",
    )(jnp.asarray(qi), jnp.asarray(kj), u, u, vt, u, bias, jnp.asarray(slopes), lamv, gn)


def _attn_sample_kernel(pt_ref, q_ref, z_ref, kn_ref, vn_ref, *rest, pps, heads, hd, tq, page, lam_init):
    kp_refs = rest[:pps]
    vp_refs = rest[pps:2 * pps]
    (bp_ref, bn_ref, slope_ref, lamv_ref, gn_ref, o_ref, qb_ref, s_ref, m_ref, l_ref, acc_ref) = rest[2 * pps:]
    j = pl.program_id(1)
    e = 2 * hd
    nr = heads * tq
    pw = page * heads

    @pl.when(j == 0)
    def _():
        q = q_ref[...].astype(F32) * (hd ** -0.5)
        lane = lax.broadcasted_iota(jnp.int32, (tq, e), 1)
        for hh in range(heads):
            qh = q[:, hh * e:(hh + 1) * e]
            qb_ref[hh * tq:(hh + 1) * tq, :] = jnp.where(lane < hd, qh, 0.0)
            qb_ref[nr + hh * tq:nr + (hh + 1) * tq, :] = jnp.where(lane >= hd, qh, 0.0)
        m_ref[...] = jnp.full(m_ref.shape, NEG, F32)
        l_ref[...] = jnp.zeros(l_ref.shape, F32)
        acc_ref[...] = jnp.zeros(acc_ref.shape, F32)

    qb = qb_ref[...].astype(BF16)
    shifts = [slope_ref[...] * ((j * pps + s) * page).astype(F32) for s in range(pps)]
    mx = None
    for s in range(pps):
        kp = kp_refs[s][...].reshape(pw, e).astype(BF16)
        sc = lax.dot_general(qb, kp, _NT, preferred_element_type=F32) + bp_ref[...]
        s_ref[:, s * pw:(s + 1) * pw] = sc
        ms = jnp.max(sc, axis=-1, keepdims=True) + shifts[s]
        mx = ms if mx is None else jnp.maximum(mx, ms)
    m_old = m_ref[...]
    m_new = jnp.maximum(m_old, mx)
    alpha = jnp.exp(m_old - m_new)
    lsum = jnp.zeros_like(m_old)
    pv = jnp.zeros(acc_ref.shape, F32)
    for s in range(pps):
        pr = jnp.exp(s_ref[:, s * pw:(s + 1) * pw] - (m_new - shifts[s]))
        lsum = lsum + jnp.sum(pr, axis=-1, keepdims=True)
        vp = vp_refs[s][...].reshape(pw, e).astype(BF16)
        pv = pv + jnp.dot(pr.astype(BF16), vp, preferred_element_type=F32)
    l_ref[...] = alpha * l_ref[...] + lsum
    acc_ref[...] = alpha * acc_ref[...] + pv
    m_ref[...] = m_new

    @pl.when(j == pl.num_programs(1) - 1)
    def _():
        kn = kn_ref[...]
        vn = vn_ref[...]
        knr = jnp.concatenate([kn[:, hh * e:(hh + 1) * e] for hh in range(heads)], axis=0).astype(BF16)
        vnr = jnp.concatenate([vn[:, hh * e:(hh + 1) * e] for hh in range(heads)], axis=0).astype(BF16)
        sc = lax.dot_general(qb, knr, _NT, preferred_element_type=F32) + bn_ref[...]
        _softmax_step(sc, 0.0, vnr, m_ref, l_ref, acc_ref)
        o = acc_ref[...] / l_ref[...]
        lam = _lambda_value(lamv_ref, lam_init)
        att = o[0:nr] - lam * o[nr:2 * nr]
        z = z_ref[...].astype(F32)
        for hh in range(heads):
            r = _head_post(att[hh * tq:(hh + 1) * tq], gn_ref[hh:hh + 1, :], lam_init,
                           z[:, hh * e:(hh + 1) * e])
            o_ref[:, hh * e:(hh + 1) * e] = r.astype(o_ref.dtype)


def _attn_sample_call(u, k_new, v_new, cache_k, cache_v, page_table, lamv, gn, *, tq, heads, hd,
                      lam_init, cols):
    bd, npg = page_table.shape
    page = cache_k.shape[1]
    pps = PAGES_PER_STEP
    e = 2 * hd
    nr = heads * tq
    past = npg * page
    slopes = 2.0 ** (-8.0 * np.arange(1, heads + 1, dtype=np.float32) / heads)
    r_head = (np.arange(2 * nr) % nr) // tq
    r_tok = np.arange(2 * nr) % tq
    r_slope = slopes[r_head]
    c_tok, c_head = np.arange(page * heads) // heads, np.arange(page * heads) % heads
    bp = r_slope[:, None] * (c_tok[None, :] - (past + r_tok)[:, None])
    bp = np.where(r_head[:, None] == c_head[None, :], bp, NEG).astype(np.float32)
    n_head, n_tok = np.arange(nr) // tq, np.arange(nr) % tq
    bn = r_slope[:, None] * (n_tok[None, :] - r_tok[:, None])
    ok = (r_head[:, None] == n_head[None, :]) & (n_tok[None, :] <= r_tok[:, None])
    bn = np.where(ok, bn, NEG).astype(np.float32)
    slope_col = r_slope.reshape(2 * nr, 1).astype(np.float32)
    w = heads * e

    def page_spec(s):
        return pl.BlockSpec((None, page, heads, e),
                            lambda b, j, pt: (pt[b * npg + j * pps + s], 0, 0, 0))

    const2 = lambda b, j, pt: (0, 0)
    grid_spec = pltpu.PrefetchScalarGridSpec(
        num_scalar_prefetch=1,
        grid=(bd, npg // pps),
        in_specs=[pl.BlockSpec((tq, w), lambda b, j, pt: (b, cols["q_a"] // w)),
                  pl.BlockSpec((tq, w), lambda b, j, pt: (b, cols["z_a"] // w)),
                  pl.BlockSpec((tq, w), lambda b, j, pt: (b, 0)),
                  pl.BlockSpec((tq, w), lambda b, j, pt: (b, 0))]
                 + [page_spec(s) for s in range(pps)] + [page_spec(s) for s in range(pps)]
                 + [pl.BlockSpec(bp.shape, const2), pl.BlockSpec(bn.shape, const2),
                    pl.BlockSpec(slope_col.shape, const2), pl.BlockSpec(lamv.shape, const2),
                    pl.BlockSpec(gn.shape, const2)],
        out_specs=pl.BlockSpec((tq, w), lambda b, j, pt: (b, 0)),
        scratch_shapes=[pltpu.VMEM((2 * nr, e), F32), pltpu.VMEM((2 * nr, pps * page * heads), F32),
                        pltpu.VMEM((2 * nr, 1), F32), pltpu.VMEM((2 * nr, 1), F32),
                        pltpu.VMEM((2 * nr, e), F32)],
    )
    return pl.pallas_call(
        functools.partial(_attn_sample_kernel, pps=pps, heads=heads, hd=hd, tq=tq, page=page,
                          lam_init=lam_init),
        grid_spec=grid_spec,
        out_shape=jax.ShapeDtypeStruct((bd * tq, w), F32),
        compiler_params=_cparams(("parallel", "arbitrary")),
        name="attn_sample",
    )(page_table.reshape(-1), u, u, k_new, v_new, *([cache_k] * pps), *([cache_v] * pps),
      jnp.asarray(bp), jnp.asarray(bn), jnp.asarray(slope_col), lamv, gn)


def _mlstm_chunks(chains, scale):
    ln = chains[0][0].shape[0]
    row = lax.broadcasted_iota(jnp.int32, (ln, ln), 0)
    col = lax.broadcasted_iota(jnp.int32, (ln, ln), 1)
    tri = col <= row
    eye = col == row

    gate = []
    for (q, k, v, ig_row, lf_row, c_st, n_st, m_st) in chains:
        lf_b = jnp.broadcast_to(lf_row, (ln, ln))
        ig_b = jnp.broadcast_to(ig_row, (ln, ln))
        b_col = jnp.sum(jnp.where(tri, lf_b, 0.0), axis=1, keepdims=True)
        lf_col = jnp.sum(jnp.where(eye, lf_b, 0.0), axis=1, keepdims=True)
        ig_col = jnp.sum(jnp.where(eye, ig_b, 0.0), axis=1, keepdims=True)
        b_row = jnp.sum(jnp.where(row <= col, jnp.broadcast_to(lf_col, (ln, ln)), 0.0),
                        axis=0, keepdims=True)
        dmat = jnp.where(tri, b_col - b_row + ig_row, NEG)
        a_col = b_col + m_st
        m_t = jnp.maximum(a_col, jnp.max(dmat, axis=1, keepdims=True))
        m_new = m_t[ln - 1:ln, :]
        b_last = b_col[ln - 1:ln, :]
        wk = jnp.exp(b_last - b_col + ig_col - m_new)
        decay = jnp.exp(b_last + m_st - m_new)
        gate.append((jnp.exp(dmat - m_t), jnp.exp(a_col - m_t), m_t, m_new, wk, decay))

    first = []
    for (q, k, v, _, _, c_st, _, _), (_, _, _, _, wk, _) in zip(chains, gate):
        qk = lax.dot_general(q, k, _NT, preferred_element_type=F32)
        q_c = lax.dot_general(q, c_st.astype(BF16), _NT, preferred_element_type=F32)
        vw = (v.astype(F32) * wk).astype(BF16)
        vk = lax.dot_general(vw, k, _TN, preferred_element_type=F32)
        first.append((qk, q_c, vk))

    out = []
    for (q, k, v, _, _, c_st, n_st, _), (dexp, inter, m_t, m_new, wk, decay), (qk, q_c, vk) in zip(
            chains, gate, first):
        s = qk * scale * dexp
        num = inter * q_c + jnp.dot(s.astype(BF16), v, preferred_element_type=F32)
        den = (inter * jnp.sum(q.astype(F32) * n_st, axis=1, keepdims=True)
               + jnp.sum(s, axis=1, keepdims=True))
        h = num / jnp.maximum(jnp.abs(den), jnp.exp(-m_t))
        c_new = decay * c_st + vk * scale
        n_new = decay * n_st + jnp.sum(k.astype(F32) * wk, axis=0, keepdims=True) * scale
        out.append((h, c_new, n_new, m_new))
    return out


def _log_sigmoid(x):
    return jnp.minimum(x, 0.0) - jnp.log(1.0 + jnp.exp(-jnp.abs(x)))


def _mlstm_post(h, gain, o, z):
    hn = h * lax.rsqrt(jnp.mean(h * h, axis=-1, keepdims=True) + EPS) * gain
    return _sigmoid(o) * hn * _silu(z)


def _mlstm_prompt_kernel(gb_ref, q_ref, k_ref, v_ref, o_ref, z_ref, g_ref, gn_ref, c0_ref, n0_ref, m0_ref,
                         y_ref, c_out, n_out, m_out, c_s, n_s, m_s, *, scale, heads, dk, dv):
    ci = pl.program_id(1)

    @pl.when(ci == 0)
    def _():
        c_s[...] = c0_ref[...]
        n_s[...] = n0_ref[...]
        m_s[...] = m0_ref[...]

    g = g_ref[...]
    chains = []
    for hh in range(heads):
        ks = slice(hh * dk, (hh + 1) * dk)
        vs = slice(hh * dv, (hh + 1) * dv)
        ig = g[hh:hh + 1, :] + gb_ref[0, hh]
        lf = _log_sigmoid(g[heads + hh:heads + hh + 1, :] + gb_ref[1, hh])
        chains.append((q_ref[:, ks], k_ref[:, ks], v_ref[:, vs], ig, lf, c_s[hh], n_s[hh], m_s[hh]))
    for hh, (h, c_new, n_new, m_new) in enumerate(_mlstm_chunks(chains, scale)):
        vs = slice(hh * dv, (hh + 1) * dv)
        c_s[hh] = c_new
        n_s[hh] = n_new
        m_s[hh] = m_new
        y_ref[:, vs] = _mlstm_post(h, gn_ref[hh], o_ref[:, vs].astype(F32),
                                   z_ref[:, vs].astype(F32)).astype(y_ref.dtype)

    @pl.when(ci == pl.num_programs(1) - 1)
    def _():
        c_out[...] = c_s[...]
        n_out[...] = n_s[...]
        m_out[...] = m_s[...]


def _mlstm_prompt_call(u, gates, gb, gn, c0, n0, m0, *, batch, seq, heads, dk, dv, cols):
    ln = MLSTM_CHUNK
    nc = seq // ln
    wk, wv = heads * dk, heads * dv
    qb, kb, vb, ob, zb = (cols["q_m"] // wk, cols["k_m"] // wk, cols["v_m"] // wv, cols["o_m"] // wv,
                          cols["z_m"] // wv)
    st4 = lambda b, c: (b, 0, 0, 0)
    return pl.pallas_call(
        functools.partial(_mlstm_prompt_kernel, scale=dk ** -0.5, heads=heads, dk=dk, dv=dv),
        grid=(batch, nc),
        in_specs=[pl.BlockSpec(memory_space=pltpu.SMEM),
                  pl.BlockSpec((ln, wk), lambda b, c: (b * nc + c, qb)),
                  pl.BlockSpec((ln, wk), lambda b, c: (b * nc + c, kb)),
                  pl.BlockSpec((ln, wv), lambda b, c: (b * nc + c, vb)),
                  pl.BlockSpec((ln, wv), lambda b, c: (b * nc + c, ob)),
                  pl.BlockSpec((ln, wv), lambda b, c: (b * nc + c, zb)),
                  pl.BlockSpec((None, None, 2 * heads, ln), lambda b, c: (b, c, 0, 0)),
                  pl.BlockSpec((heads, 1, dv), lambda b, c: (0, 0, 0)),
                  pl.BlockSpec((None, heads, dv, dk), st4),
                  pl.BlockSpec((None, heads, 1, dk), st4),
                  pl.BlockSpec((None, heads, 1, 1), st4)],
        out_specs=[pl.BlockSpec((ln, wv), lambda b, c: (b * nc + c, 0)),
                   pl.BlockSpec((None, heads, dv, dk), st4),
                   pl.BlockSpec((None, heads, 1, dk), st4),
                   pl.BlockSpec((None, heads, 1, 1), st4)],
        out_shape=[jax.ShapeDtypeStruct((batch * seq, wv), BF16),
                   jax.ShapeDtypeStruct((batch, heads, dv, dk), F32),
                   jax.ShapeDtypeStruct((batch, heads, 1, dk), F32),
                   jax.ShapeDtypeStruct((batch, heads, 1, 1), F32)],
        scratch_shapes=[pltpu.VMEM((heads, dv, dk), F32), pltpu.VMEM((heads, 1, dk), F32),
                        pltpu.VMEM((heads, 1, 1), F32)],
        compiler_params=_cparams(("parallel", "arbitrary")),
        name="mlstm_prompt",
    )(gb, u, u, u, u, u, gates, gn, c0, n0, m0)


def _mlstm_sample_kernel(gb_ref, q_ref, k_ref, v_ref, o_ref, z_ref, g_ref, gn_ref, c0_ref, n0_ref, m0_ref,
                         y_ref, c_out, n_out, m_out, *, scale, bb, tq):
    hh = pl.program_id(1)
    q = q_ref[...].astype(F32)
    k = k_ref[...].astype(F32)
    v = v_ref[...].astype(F32)
    o = o_ref[...].astype(F32)
    z = z_ref[...].astype(F32)
    gain = gn_ref[...]
    chains = []
    for i in range(bb):
        rs = slice(i * tq, (i + 1) * tq)
        g = g_ref[i]
        ig = g[0:1, :] + gb_ref[0, hh]
        lf = _log_sigmoid(g[1:2, :] + gb_ref[1, hh])
        chains.append((q[rs].astype(BF16), k[rs].astype(BF16), v[rs].astype(BF16), ig, lf,
                       c0_ref[i], n0_ref[i], m0_ref[i]))
    for i, (h, c_new, n_new, m_new) in enumerate(_mlstm_chunks(chains, scale)):
        rs = slice(i * tq, (i + 1) * tq)
        y_ref[rs, :] = _mlstm_post(h, gain, o[rs], z[rs]).astype(y_ref.dtype)
        c_out[i] = c_new
        n_out[i] = n_new
        m_out[i] = m_new


def _mlstm_sample_call(u, gates, gb, gn, c0, n0, m0, *, tq, heads, dk, dv, cols):
    bd = c0.shape[0]
    bb = MLSTM_DEC_BB
    assert bd % bb == 0, (bd, bb)
    qb, kb, vb, ob, zb = (cols["q_m"] // dk, cols["k_m"] // dk, cols["v_m"] // dv, cols["o_m"] // dv,
                          cols["z_m"] // dv)
    st4 = lambda g, h: (g, h, 0, 0)
    return pl.pallas_call(
        functools.partial(_mlstm_sample_kernel, scale=dk ** -0.5, bb=bb, tq=tq),
        grid=(bd // bb, heads),
        in_specs=[pl.BlockSpec(memory_space=pltpu.SMEM),
                  pl.BlockSpec((bb * tq, dk), lambda g, h: (g, qb + h)),
                  pl.BlockSpec((bb * tq, dk), lambda g, h: (g, kb + h)),
                  pl.BlockSpec((bb * tq, dv), lambda g, h: (g, vb + h)),
                  pl.BlockSpec((bb * tq, dv), lambda g, h: (g, ob + h)),
                  pl.BlockSpec((bb * tq, dv), lambda g, h: (g, zb + h)),
                  pl.BlockSpec((bb, None, 2, tq), st4),
                  pl.BlockSpec((None, 1, dv), lambda g, h: (h, 0, 0)),
                  pl.BlockSpec((bb, None, dv, dk), st4),
                  pl.BlockSpec((bb, None, 1, dk), st4),
                  pl.BlockSpec((bb, None, 1, 1), st4)],
        out_specs=[pl.BlockSpec((bb * tq, dv), lambda g, h: (g, h)),
                   pl.BlockSpec((bb, None, dv, dk), st4),
                   pl.BlockSpec((bb, None, 1, dk), st4),
                   pl.BlockSpec((bb, None, 1, 1), st4)],
        out_shape=[jax.ShapeDtypeStruct((bd * tq, heads * dv), F32),
                   jax.ShapeDtypeStruct((bd, heads, dv, dk), F32),
                   jax.ShapeDtypeStruct((bd, heads, 1, dk), F32),
                   jax.ShapeDtypeStruct((bd, heads, 1, 1), F32)],
        compiler_params=_cparams(("parallel", "parallel")),
        name="mlstm_sample",
    )(gb, u, u, u, u, u, gates, gn, c0, n0, m0)


def _out_kernel(x_ref, ya_ref, ym_ref, ga_ref, gm_ref, gate_ref, np_ref, wa_ref, wm_ref, wo_ref, o_ref,
                *, rowwise):
    ya = jnp.dot(ya_ref[...].astype(BF16), wa_ref[...], preferred_element_type=F32)
    ym = jnp.dot(ym_ref[...].astype(BF16), wm_ref[...], preferred_element_type=F32)
    y2 = _sigmoid(ga_ref[...].astype(F32)) * ya + _sigmoid(gm_ref[...].astype(F32)) * ym
    y = jnp.dot(y2.astype(BF16), wo_ref[...], preferred_element_type=F32)
    yn = y * lax.rsqrt(jnp.mean(y * y, axis=-1, keepdims=True) + EPS) * np_ref[...]
    if rowwise:
        g, d = gate_ref.shape
        gy = (gate_ref[...][:, None, :] * yn.reshape(g, yn.shape[0] // g, d)).reshape(yn.shape)
    else:
        gy = gate_ref[0] * yn
    o_ref[...] = x_ref[...] + gy


def _out_call(x, ya, ym, u, gate, norm_post, wa, wm, wo, *, rowwise, rows_per_group, tm, cols):
    t, d = x.shape
    da = ya.shape[1]
    ga_blk, gm_blk = cols["g_a"] // d, cols["g_m"] // d
    if rowwise:
        gspec = pl.BlockSpec((tm // rows_per_group, d), lambda i: (i, 0))
    else:
        gspec = pl.BlockSpec((1, 1, d), lambda i: ((i * tm) // rows_per_group, 0, 0))
    resident = dict(pipeline_mode=pl.Buffered(1))
    return pl.pallas_call(
        functools.partial(_out_kernel, rowwise=rowwise),
        grid=(t // tm,),
        in_specs=[pl.BlockSpec((tm, d), lambda i: (i, 0)),
                  pl.BlockSpec((tm, da), lambda i: (i, 0)),
                  pl.BlockSpec((tm, da), lambda i: (i, 0)),
                  pl.BlockSpec((tm, d), lambda i: (i, ga_blk)),
                  pl.BlockSpec((tm, d), lambda i: (i, gm_blk)),
                  gspec,
                  pl.BlockSpec((1, d), lambda i: (0, 0)),
                  pl.BlockSpec(wa.shape, lambda i: (0, 0), **resident),
                  pl.BlockSpec(wm.shape, lambda i: (0, 0), **resident),
                  pl.BlockSpec(wo.shape, lambda i: (0, 0), **resident)],
        out_specs=pl.BlockSpec((tm, d), lambda i: (i, 0)),
        out_shape=jax.ShapeDtypeStruct((t, d), F32),
        compiler_params=_cparams(("parallel",)),
        name="out_stage",
    )(x, ya, ym, u, u, gate, norm_post, wa, wm, wo)


def kernel(x_prompt, x_sample, cache_k, cache_v, state_C, state_n, state_m, page_table, c_prompt, c_sample,
           w_ada, b_ada, norm_pre, norm_post, w_in, b_igate, b_fgate, lambda_q1, lambda_k1, lambda_q2,
           lambda_k2, attn_head_norm, mlstm_head_norm, w_br_a, w_br_m, w_out):
    depth = w_in.shape[0]
    assert depth == 1, "single-layer trunk"
    batch, seq, d = x_prompt.shape
    bd, tq, _ = x_sample.shape
    heads_a, e = cache_k.shape[3], cache_k.shape[4]
    hd = e // 2
    d_att = heads_a * e
    heads_m, dv, dk = state_C.shape[2], state_C.shape[3], state_C.shape[4]
    d_mv, d_mqk = heads_m * dv, heads_m * dk
    layer = 0
    lam_init = 0.8 - 0.6 * math.exp(-0.3 * layer)

    names = ("q_a", "k_a", "v_a", "z_a", "q_m", "k_m", "v_m", "o_m", "z_m", "i_m", "f_m", "g_a", "g_m")
    widths = (d_att, d_att, d_att, d_att, d_mqk, d_mqk, d_mv, d_mv, d_mv, heads_m, heads_m, d, d)
    src = dict(zip(names, np.concatenate([[0], np.cumsum(widths)[:-1]]).tolist()))
    main_cols = src["i_m"]
    cols = {nm: src[nm] for nm in names[:9]}
    cols["g_a"], cols["g_m"] = main_cols, main_cols + d

    wt = w_in.reshape(w_in.shape[1:]).T
    inproj = functools.partial(_inproj_call, main_cols=main_cols, gate_cols=2 * heads_m, k_col=src["k_a"],
                               v_col=src["v_a"], kv_cols=d_att, tn=512)
    wa = w_br_a[layer].astype(BF16)
    wm = w_br_m[layer].astype(BF16)
    wo = w_out[layer].astype(BF16)

    n_c = batch + bd
    pad = (-n_c) % 8
    c_all = jnp.concatenate([c_prompt, c_sample, jnp.zeros((pad, d), F32)], axis=0)
    mod = _mod_call(c_all, w_ada[layer], b_ada[layer].reshape(1, -1))
    shift, scale, gate = mod[:, :d], mod[:, d:2 * d], mod[:, 2 * d:]

    lamv = jnp.stack([lambda_q1[layer], lambda_k1[layer], lambda_q2[layer], lambda_k2[layer]])
    gn_a = attn_head_norm[layer].reshape(heads_a, e)
    gn_m = mlstm_head_norm[layer].reshape(heads_m, 1, dv)
    gb = jnp.stack([b_igate[layer], b_fgate[layer]])
    npre = norm_pre[layer].reshape(1, d)
    npost = norm_post[layer].reshape(1, d)

    xp = x_prompt.reshape(batch * seq, d)
    grp = lambda a: a[:batch].reshape(batch, 1, d)
    u_p, k_p, v_p, g_p, vt_p = inproj(xp, grp(scale), grp(shift), npre, wt, rowwise=False,
                                      rows_per_group=seq, tm=1024, u_dtype=BF16, want_vt=True, x_buffers=2)
    ya_p = _attn_prompt_call(u_p, vt_p, lamv, gn_a.reshape(heads_a, e, 1), batch=batch, seq=seq,
                             heads=heads_a, hd=hd, lam_init=lam_init, cols=cols)
    nc = seq // MLSTM_CHUNK
    gates_p = g_p.reshape(batch, nc, MLSTM_CHUNK, 2 * heads_m).transpose(0, 1, 3, 2)
    ym_p, c_p, n_p, m_p = _mlstm_prompt_call(
        u_p, gates_p, gb, gn_m, jnp.zeros((batch, heads_m, dv, dk), F32), jnp.zeros((batch, heads_m, 1, dk), F32),
        jnp.zeros((batch, heads_m, 1, 1), F32), batch=batch, seq=seq, heads=heads_m, dk=dk, dv=dv, cols=cols)
    y_p = _out_call(xp, ya_p, ym_p, u_p, grp(gate), npost, wa, wm, wo, rowwise=False, rows_per_group=seq,
                    tm=512, cols=cols)

    xs = x_sample.reshape(bd * tq, d)
    rep = lambda a: a[batch:batch + bd]
    u_s, k_s, v_s, g_s = inproj(xs, rep(scale), rep(shift), npre, wt, rowwise=True,
                                rows_per_group=tq, tm=min(1024, bd * tq), u_dtype=F32, want_vt=False,
                                x_buffers=1)
    ya_s = _attn_sample_call(u_s, k_s, v_s, cache_k.reshape(cache_k.shape[1:]),
                             cache_v.reshape(cache_v.shape[1:]), page_table, lamv, gn_a,
                             tq=tq, heads=heads_a, hd=hd, lam_init=lam_init, cols=cols)
    gates_s = g_s.reshape(bd, tq, 2, heads_m).transpose(0, 3, 2, 1)
    ym_s, c_s, n_s, m_s = _mlstm_sample_call(
        u_s, gates_s, gb, gn_m, state_C.reshape(state_C.shape[1:]), state_n.reshape(bd, heads_m, 1, dk),
        state_m.reshape(bd, heads_m, 1, 1), tq=tq, heads=heads_m, dk=dk, dv=dv, cols=cols)
    y_s = _out_call(xs, ya_s, ym_s, u_s, rep(gate), npost, wa, wm, wo, rowwise=True, rows_per_group=tq,
                    tm=min(256, bd * tq), cols=cols)

    return (y_p.reshape(batch, seq, d), y_s.reshape(bd, tq, d),
            k_p.reshape(1, batch, seq, heads_a, e), v_p.reshape(1, batch, seq, heads_a, e),
            c_p.reshape(1, batch, heads_m, dv, dk), n_p.reshape(1, batch, heads_m, dk),
            m_p.reshape(1, batch, heads_m),
            k_s.reshape(1, bd, tq, heads_a, e), v_s.reshape(1, bd, tq, heads_a, e),
            c_s.reshape(1, bd, heads_m, dv, dk), n_s.reshape(1, bd, heads_m, dk), m_s.reshape(1, bd, heads_m))
```
